```python
import math
import jax
import jax.numpy as jnp
from jax import lax
import numpy as np

D_MODEL = 1024
BATCH = 8
SEQ = 2048
DEPTH = 2

GRID_W = 64
CTX_LEN = 256

NORM_EPS = 1e-6
N_BRANCH = 3
N_MOD = 6

DA_HEADS = 4
DA_HEAD_DIM = 64
DA_V_DIM = 2 * DA_HEAD_DIM
DA_QK_COLS = DA_HEADS * 2 * DA_HEAD_DIM
DA_WIDTH = DA_HEADS * DA_V_DIM
DA_SUBLN_EPS = 1e-5
Q_BLOCK = 128
ROPE_BASE = 10000.0

SC_WIDTH = 512
SC_KSIZE = 3

RW_HEADS = 8
RW_HEAD_DIM = 64
RW_WIDTH = RW_HEADS * RW_HEAD_DIM
RW_DECAY_LORA = 64
RW_AAA_LORA = 64
RW_GATE_LORA = 160
RW_DECAY_SCALE = 0.606531
RW_GN_EPS = 64e-5
N_DIR = 2

N_EXPERTS = 32
TOP_K = 4
D_FF = 1024
SWIGLU_LIMIT = 7.0
SWIGLU_ALPHA = 1.702
MOE_BLOCK = 256

BRANCH_WIDTH = DA_WIDTH
IN_SPLITS = (DA_QK_COLS, DA_QK_COLS, DA_WIDTH, SC_WIDTH, SC_WIDTH, SC_WIDTH, RW_WIDTH, RW_WIDTH, RW_WIDTH, N_BRANCH * D_MODEL)
IN_COLS = sum(IN_SPLITS)

kernel_name = 'hybrid_diffattn_conv_rwkv7_moe_dit'


def rms_norm(x, g, eps=NORM_EPS):
    xf = x.astype(jnp.float32)
    y = xf * lax.rsqrt(jnp.mean(xf * xf, axis=-1, keepdims=True) + eps)
    return (y * g.astype(jnp.float32)).astype(x.dtype)


def modulate(x, g, shift, scale):
    return rms_norm(x, g) * (1 + scale) + shift


def split_cols(t):
    offsets = np.cumsum(IN_SPLITS)[:-1].tolist()
    return jnp.split(t, offsets, axis=-1)


def to_heads(t, n_heads):
    return t.reshape(t.shape[:-1] + (n_heads, t.shape[-1] // n_heads))


def rope_1d(t, pos):
    half = t.shape[-1] // 2
    inv = jnp.power(ROPE_BASE, -jnp.arange(half, dtype=jnp.float32) / half)
    ang = pos.astype(jnp.float32)[:, None] * inv[None, :]
    cos = jnp.cos(ang)[None, :, None, :]
    sin = jnp.sin(ang)[None, :, None, :]
    t1 = t[..., :half].astype(jnp.float32)
    t2 = t[..., half:].astype(jnp.float32)
    return jnp.concatenate([t1 * cos - t2 * sin, t1 * sin + t2 * cos], axis=-1).astype(t.dtype)


def rope_2d(t, rows, cols):
    d = t.shape[-1] // 2
    return jnp.concatenate([rope_1d(t[..., :d], rows), rope_1d(t[..., d:], cols)], axis=-1)


def qk_pair(t):
    t = t.reshape(t.shape[:-1] + (DA_HEADS, 2, DA_HEAD_DIM))
    return t[..., 0, :], t[..., 1, :]


def diff_attend(q1, q2, k1, k2, v, lam):
    scale = DA_HEAD_DIM ** -0.5
    s1 = jnp.einsum('bqhd,bkhd->bhqk', q1, k1).astype(jnp.float32) * scale
    s2 = jnp.einsum('bqhd,bkhd->bhqk', q2, k2).astype(jnp.float32) * scale
    p = jax.nn.softmax(s1, axis=-1) - lam * jax.nn.softmax(s2, axis=-1)
    return jnp.einsum('bhqk,bkhe->bqhe', p.astype(v.dtype), v)


def blocked_diff_attend(q1, q2, k1, k2, v, lam):
    b, s, h, d = q1.shape
    nb = s // Q_BLOCK

    def to_blocks(t):
        return t.reshape(b, nb, Q_BLOCK, h, d).swapaxes(0, 1)

    def one_block(qs):
        return diff_attend(qs[0], qs[1], k1, k2, v, lam)

    out = lax.map(one_block, (to_blocks(q1), to_blocks(q2)))
    return out.swapaxes(0, 1).reshape(b, s, h, v.shape[-1])


def diff_attention_branch(q, k, v, qc, kc, vc, rows, cols, lam, lam_init, subln_g, need_ctx_out):
    q1, q2 = qk_pair(q)
    k1, k2 = qk_pair(k)
    q1, q2, k1, k2 = [rope_2d(t, rows, cols) for t in (q1, q2, k1, k2)]
    k1c, k2c = qk_pair(kc)
    vh = to_heads(v, DA_HEADS)
    vhc = to_heads(vc, DA_HEADS)
    o = blocked_diff_attend(q1, q2, jnp.concatenate([k1, k1c], axis=1), jnp.concatenate([k2, k2c], axis=1),
                            jnp.concatenate([vh, vhc], axis=1), lam)
    out = (rms_norm(o, subln_g, DA_SUBLN_EPS) * (1 - lam_init)).reshape(o.shape[:2] + (DA_WIDTH,))
    out_c = None
    if need_ctx_out:
        q1c, q2c = qk_pair(qc)
        oc = diff_attend(q1c, q2c, k1c, k2c, vhc, lam)
        out_c = (rms_norm(oc, subln_g, DA_SUBLN_EPS) * (1 - lam_init)).reshape(oc.shape[:2] + (DA_WIDTH,))
    return out, out_c


def depthwise_conv3(u, w):
    return lax.conv_general_dilated(u, w[:, None, :].astype(u.dtype), window_strides=(1,), padding=((1, 1),),
                                    dimension_numbers=('NWC', 'WIO', 'NWC'), feature_group_count=u.shape[-1])


def short_conv_branch(b_gate, c_gate, xin, w):
    return b_gate * depthwise_conv3(c_gate * xin, w)


def l2_normalize(t):
    tf = t.astype(jnp.float32)
    return (tf * lax.rsqrt(jnp.maximum(jnp.sum(tf * tf, axis=-1, keepdims=True), 1e-24))).astype(t.dtype)


def rwkv_direction_terms(h, k, p, d):
    decay = jnp.exp(-RW_DECAY_SCALE * jax.nn.sigmoid(p['rw_w0'][d] + jnp.tanh(h @ p['rw_w1'][d]) @ p['rw_w2'][d]))
    a = jax.nn.sigmoid(p['rw_a0'][d] + (h @ p['rw_a1'][d]) @ p['rw_a2'][d])
    k_eff = k * (1 + (a - 1) * p['rw_k_a'])
    return to_heads(decay, RW_HEADS), to_heads(a, RW_HEADS), to_heads(k_eff, RW_HEADS)


def rwkv_scan(state0, r, decay, k, v, kk, a, reverse, emit):
    xs = tuple(t.swapaxes(0, 1) for t in (r, decay, k, v, kk, a))

    def step(s, inp):
        r_t, w_t, k_t, v_t, kk_t, a_t = inp
        s = (s * w_t[:, :, None, :]
             - jnp.einsum('bhij,bhj->bhi', s, kk_t)[..., None] * (kk_t * a_t)[:, :, None, :]
             + v_t[..., None] * k_t[:, :, None, :])
        y = jnp.einsum('bhij,bhj->bhi', s, r_t) if emit else None
        return s, y

    s_final, ys = lax.scan(step, state0, xs, reverse=reverse)
    return s_final, (ys.swapaxes(0, 1) if emit else None)


def bonus_term(r, k, v, r_k):
    return jnp.sum(r * k * r_k, axis=-1, keepdims=True) * v


def rwkv_output(y, bonus, g, p):
    yf = y.astype(jnp.float32)
    mu = jnp.mean(yf, axis=-1, keepdims=True)
    var = jnp.mean(jnp.square(yf - mu), axis=-1, keepdims=True)
    yn = ((yf - mu) * lax.rsqrt(var + RW_GN_EPS)).reshape(y.shape[:2] + (RW_WIDTH,))
    yn = (yn * p['rw_lnx_g'].astype(jnp.float32) + p['rw_lnx_b'].astype(jnp.float32)).astype(y.dtype)
    return (yn + bonus.reshape(y.shape[:2] + (RW_WIDTH,))) * g


def rwkv_branch(h, r, k, v, hc, rc, kc, vc, p, need_ctx_out):
    b = h.shape[0]
    state0 = jnp.zeros((b, RW_HEADS, RW_HEAD_DIM, RW_HEAD_DIM), h.dtype)
    rh, vh = to_heads(r, RW_HEADS), to_heads(v, RW_HEADS)
    rch, vch = to_heads(rc, RW_HEADS), to_heads(vc, RW_HEADS)
    kk = l2_normalize(to_heads(k * p['rw_k_k'], RW_HEADS))
    kkc = l2_normalize(to_heads(kc * p['rw_k_k'], RW_HEADS))
    ys, bon, ycs, bonc = [], [], [], []
    for d in range(N_DIR):
        rev = d == 1
        dc, ac, kec = rwkv_direction_terms(hc, kc, p, d)
        s_ctx, yc_d = rwkv_scan(state0, rch, dc, kec, vch, kkc, ac, rev, need_ctx_out)
        dl, al, kel = rwkv_direction_terms(h, k, p, d)
        _, y_d = rwkv_scan(s_ctx, rh, dl, kel, vh, kk, al, rev, True)
        ys.append(y_d)
        bon.append(bonus_term(rh, kel, vh, p['rw_r_k']))
        if need_ctx_out:
            ycs.append(yc_d)
            bonc.append(bonus_term(rch, kec, vch, p['rw_r_k']))
    g = jax.nn.sigmoid(h @ p['rw_g1']) @ p['rw_g2']
    out = rwkv_output(ys[0] + ys[1], bon[0] + bon[1], g, p)
    out_c = None
    if need_ctx_out:
        gc = jax.nn.sigmoid(hc @ p['rw_g1']) @ p['rw_g2']
        out_c = rwkv_output(ycs[0] + ycs[1], bonc[0] + bonc[1], gc, p)
    return out, out_c


def merge_branches(branches, gate_logits, w_branch, w_out):
    stacked = jnp.stack(branches, axis=-2)
    proj = jnp.einsum('btne,ned->btnd', stacked, w_branch)
    gates = jax.nn.sigmoid(gate_logits.reshape(gate_logits.shape[:-1] + (N_BRANCH, D_MODEL)))
    return jnp.sum(gates * proj, axis=-2) @ w_out


def token_mixer(h, hc, rows, cols, lam, lam_init, p, need_ctx_out):
    q, k, v, sb, sg, sx, rr, rk, rv, gl = split_cols(h @ p['w_in'])
    qc, kc, vc, sbc, sgc, sxc, rrc, rkc, rvc, glc = split_cols(hc @ p['w_in'])
    o_da, o_da_c = diff_attention_branch(q, k, v, qc, kc, vc, rows, cols, lam, lam_init, p['da_subln_g'], need_ctx_out)
    o_rw, o_rw_c = rwkv_branch(h, rr, rk, rv, hc, rrc, rkc, rvc, p, need_ctx_out)
    o_sc = short_conv_branch(sb, sg, sx, p['conv_w'])
    y = merge_branches((o_da, o_sc, o_rw), gl, p['w_branch'], p['w_out'])
    yc = None
    if need_ctx_out:
        o_sc_c = short_conv_branch(sbc, sgc, sxc, p['conv_w'])
        yc = merge_branches((o_da_c, o_sc_c, o_rw_c), glc, p['w_branch'], p['w_out'])
    return y, yc


def moe_ffn(tokens, router_w, router_b, w1, b1, w2, b2):
    n_tok, d_model = tokens.shape
    logits = (tokens @ router_w + router_b).astype(jnp.float32)
    top_val, top_idx = lax.top_k(logits, TOP_K)
    gates = jax.nn.softmax(top_val, axis=-1)
    n_assign = n_tok * TOP_K
    expert_of = top_idx.reshape(-1).astype(jnp.int32)
    token_of = jnp.arange(n_assign, dtype=jnp.int32) // TOP_K
    order = jnp.argsort(expert_of)
    e_sorted = expert_of[order]
    counts = jnp.bincount(expert_of, length=N_EXPERTS)
    padded = (counts + MOE_BLOCK - 1) // MOE_BLOCK * MOE_BLOCK
    start = jnp.cumsum(counts) - counts
    end_padded = jnp.cumsum(padded)
    start_padded = end_padded - padded
    dest = start_padded[e_sorted] + jnp.arange(n_assign, dtype=jnp.int32) - start[e_sorted]
    n_rows = -(-n_assign // MOE_BLOCK) * MOE_BLOCK + N_EXPERTS * MOE_BLOCK
    n_blocks = n_rows // MOE_BLOCK
    row_token = jnp.full((n_rows,), n_tok, jnp.int32).at[dest].set(token_of[order])
    row_gate = jnp.zeros((n_rows,), jnp.float32).at[dest].set(gates.reshape(-1)[order])
    block_expert = jnp.minimum(jnp.searchsorted(end_padded, jnp.arange(n_blocks) * MOE_BLOCK, side='right'), N_EXPERTS - 1)
    tokens_pad = jnp.concatenate([tokens, jnp.zeros((1, d_model), tokens.dtype)], axis=0)
    xb = tokens_pad[row_token].reshape(n_blocks, MOE_BLOCK, d_model)

    def expert_block(args):
        xblk, e = args
        hid = xblk @ w1[e] + b1[e]
        glu = jnp.minimum(hid[:, :D_FF], SWIGLU_LIMIT)
        lin = jnp.clip(hid[:, D_FF:], -SWIGLU_LIMIT, SWIGLU_LIMIT)
        return (glu * jax.nn.sigmoid(SWIGLU_ALPHA * glu) * (lin + 1)) @ w2[e] + b2[e]

    yb = lax.map(expert_block, (xb, block_expert)).reshape(n_rows, d_model)
    yb = yb * row_gate[:, None].astype(yb.dtype)
    return jax.ops.segment_sum(yb, row_token, num_segments=n_tok + 1)[:n_tok]


def layer(x, xc, silu_c, silu_cc, rows, cols, li, need_ctx_out, p):
    b, s, _ = x.shape
    mod = (silu_c @ p['w_mod'] + p['b_mod']).reshape(b, 1, N_MOD, D_MODEL)
    modc = (silu_cc @ p['w_mod'] + p['b_mod']).reshape(N_MOD, D_MODEL)
    sh_a, sc_a, g_a, sh_f, sc_f, g_f = [mod[:, :, i] for i in range(N_MOD)]
    sh_ac, sc_ac, g_ac, sh_fc, sc_fc, g_fc = [modc[i] for i in range(N_MOD)]
    lam_init = 0.8 - 0.6 * math.exp(-0.3 * li)
    lv = p['da_lambda'].astype(jnp.float32)
    lam = jnp.exp(jnp.sum(lv[0] * lv[1])) - jnp.exp(jnp.sum(lv[2] * lv[3])) + lam_init
    h = modulate(x, p['norm1_g'], sh_a, sc_a)
    hc = modulate(xc, p['norm1_g'], sh_ac, sc_ac)
    y, yc = token_mixer(h, hc, rows, cols, lam, lam_init, p, need_ctx_out)
    x = x + g_a * y
    h2 = modulate(x, p['norm2_g'], sh_f, sc_f)
    moe_w = (p['router_w'], p['router_b'], p['exp_w1'], p['exp_b1'], p['exp_w2'], p['exp_b2'])
    if need_ctx_out:
        xc = xc + g_ac * yc
        h2c = modulate(xc, p['norm2_g'], sh_fc, sc_fc)
        f = moe_ffn(jnp.concatenate([h2.reshape(-1, D_MODEL), h2c.reshape(-1, D_MODEL)], axis=0), *moe_w)
        x = x + g_f * f[:b * s].reshape(b, s, D_MODEL)
        xc = xc + g_fc * f[b * s:].reshape(xc.shape)
    else:
        x = x + g_f * moe_ffn(h2.reshape(-1, D_MODEL), *moe_w).reshape(b, s, D_MODEL)
    return x, xc


def setup_inputs(seed: int = 0) -> dict:
    key = jax.random.key(seed)
    keys = jax.random.split(key, 40)
    D = D_MODEL

    def nrm(i, shape, scale):
        return scale * jax.random.normal(keys[i], shape, jnp.float32)

    return {
        'x': nrm(0, (BATCH, SEQ, D), 1.0),
        'c': nrm(1, (BATCH, D), 1.0),
        'ctx': nrm(2, (BATCH, CTX_LEN, D), 1.0),
        'c_ctx': nrm(3, (D,), 1.0),
        'w_mod': nrm(4, (DEPTH, D, N_MOD * D), 0.5 * D ** -0.5),
        'b_mod': nrm(5, (DEPTH, N_MOD * D), 0.02),
        'norm1_g': 1.0 + nrm(6, (DEPTH, D), 0.02),
        'norm2_g': 1.0 + nrm(7, (DEPTH, D), 0.02),
        'w_in': nrm(8, (DEPTH, D, IN_COLS), D ** -0.5),
        'da_lambda': nrm(9, (DEPTH, 4, DA_HEAD_DIM), 0.1),
        'da_subln_g': 1.0 + nrm(10, (DEPTH, DA_V_DIM), 0.02),
        'conv_w': nrm(11, (DEPTH, SC_KSIZE, SC_WIDTH), SC_KSIZE ** -0.5),
        'rw_w0': nrm(12, (DEPTH, N_DIR, RW_WIDTH), 0.5),
        'rw_w1': nrm(13, (DEPTH, N_DIR, D, RW_DECAY_LORA), D ** -0.5),
        'rw_w2': nrm(14, (DEPTH, N_DIR, RW_DECAY_LORA, RW_WIDTH), 0.5 * RW_DECAY_LORA ** -0.5),
        'rw_a0': nrm(15, (DEPTH, N_DIR, RW_WIDTH), 0.5),
        'rw_a1': nrm(16, (DEPTH, N_DIR, D, RW_AAA_LORA), D ** -0.5),
        'rw_a2': nrm(17, (DEPTH, N_DIR, RW_AAA_LORA, RW_WIDTH), 0.5 * RW_AAA_LORA ** -0.5),
        'rw_g1': nrm(18, (DEPTH, D, RW_GATE_LORA), D ** -0.5),
        'rw_g2': nrm(19, (DEPTH, RW_GATE_LORA, RW_WIDTH), RW_GATE_LORA ** -0.5),
        'rw_k_k': 0.85 + nrm(20, (DEPTH, RW_WIDTH), 0.05),
        'rw_k_a': 1.0 + nrm(21, (DEPTH, RW_WIDTH), 0.05),
        'rw_r_k': nrm(22, (DEPTH, RW_HEADS, RW_HEAD_DIM), 0.1),
        'rw_lnx_g': 1.0 + nrm(23, (DEPTH, RW_WIDTH), 0.02),
        'rw_lnx_b': nrm(24, (DEPTH, RW_WIDTH), 0.02),
        'w_branch': nrm(25, (DEPTH, N_BRANCH, BRANCH_WIDTH, D), BRANCH_WIDTH ** -0.5),
        'w_out': nrm(26, (DEPTH, D, D), D ** -0.5),
        'router_w': nrm(27, (DEPTH, D, N_EXPERTS), D ** -0.5),
        'router_b': nrm(28, (DEPTH, N_EXPERTS), 0.01),
        'exp_w1': nrm(29, (DEPTH, N_EXPERTS, D, 2 * D_FF), D ** -0.5),
        'exp_b1': nrm(30, (DEPTH, N_EXPERTS, 2 * D_FF), 0.01),
        'exp_w2': nrm(31, (DEPTH, N_EXPERTS, D_FF, D), D_FF ** -0.5),
        'exp_b2': nrm(32, (DEPTH, N_EXPERTS, D), 0.01),
        'final_g': 1.0 + nrm(33, (D,), 0.02),
    }


def reference(x, c, ctx, c_ctx, w_mod, b_mod, norm1_g, norm2_g, w_in, da_lambda, da_subln_g, conv_w,
              rw_w0, rw_w1, rw_w2, rw_a0, rw_a1, rw_a2, rw_g1, rw_g2, rw_k_k, rw_k_a, rw_r_k, rw_lnx_g, rw_lnx_b,
              w_branch, w_out, router_w, router_b, exp_w1, exp_b1, exp_w2, exp_b2, final_g):
    n_tokens = x.shape[1]
    grid_rows = n_tokens // GRID_W
    rows = jnp.repeat(jnp.arange(grid_rows, dtype=jnp.int32), GRID_W)
    cols = jnp.tile(jnp.arange(GRID_W, dtype=jnp.int32), grid_rows)
    silu_c = jax.nn.silu(c)
    silu_cc = jax.nn.silu(c_ctx)
    xc = ctx
    for li in range(DEPTH):
        p = {
            'w_mod': w_mod[li], 'b_mod': b_mod[li], 'norm1_g': norm1_g[li], 'norm2_g': norm2_g[li],
            'w_in': w_in[li], 'da_lambda': da_lambda[li], 'da_subln_g': da_subln_g[li], 'conv_w': conv_w[li],
            'rw_w0': rw_w0[li], 'rw_w1': rw_w1[li], 'rw_w2': rw_w2[li],
            'rw_a0': rw_a0[li], 'rw_a1': rw_a1[li], 'rw_a2': rw_a2[li],
            'rw_g1': rw_g1[li], 'rw_g2': rw_g2[li], 'rw_k_k': rw_k_k[li], 'rw_k_a': rw_k_a[li],
            'rw_r_k': rw_r_k[li], 'rw_lnx_g': rw_lnx_g[li], 'rw_lnx_b': rw_lnx_b[li],
            'w_branch': w_branch[li], 'w_out': w_out[li],
            'router_w': router_w[li], 'router_b': router_b[li],
            'exp_w1': exp_w1[li], 'exp_b1': exp_b1[li], 'exp_w2': exp_w2[li], 'exp_b2': exp_b2[li],
        }
        x, xc = layer(x, xc, silu_c, silu_cc, rows, cols, li, li < DEPTH - 1, p)
    return rms_norm(x, final_g)
```

```python
import functools
import math

import jax
import jax.numpy as jnp
from jax import lax
from jax.experimental import pallas as pl
from jax.experimental.pallas import tpu as pltpu

F32 = jnp.float32
BF16 = jnp.bfloat16

D_MODEL = 1024
GRID_W = 64
NORM_EPS = 1e-6
N_MOD = 6

DA_HEADS = 4
DA_HEAD_DIM = 64
DA_V_DIM = 128
DA_SUBLN_EPS = 1e-5
ROPE_BASE = 10000.0

SC_WIDTH = 512

RW_HEADS = 8
RW_HEAD_DIM = 64
RW_WIDTH = 512
RW_DECAY_SCALE = 0.606531
RW_GN_EPS = 64e-5
RW_CHUNK = 64
RW_LORA_W = 64
RW_GATE_LORA = 160
RW_GATE_PAD = 256

N_EXPERTS = 32
TOP_K = 4
D_FF = 1024
SWIGLU_LIMIT = 7.0
SWIGLU_ALPHA = 1.702
MOE_BLOCK = 256

C_GL = 0
C_Q = 3072
C_K = 3584
C_V = 4096
C_SB = 4608
C_SG = 5120
C_SX = 5632
C_RR = 6144
C_RK = 6656
C_RV = 7168
C_LORA = 7680
P_COLS = 8192

PROJ_TM = 512
PROJ_TN = 1024
ROPE_TILE = C_Q // PROJ_TN

VMEM_LIMIT = 56 * 1024 * 1024


def _cparams(n_axes):
    return pltpu.CompilerParams(dimension_semantics=("arbitrary",) * n_axes, vmem_limit_bytes=VMEM_LIMIT)


def _sigmoid(x):
    return 1.0 / (1.0 + jnp.exp(-x))


def _nt_dot(a, b):
    return lax.dot_general(a, b, (((1,), (1,)), ((), ())), preferred_element_type=F32)


def _tn_dot(a, b):
    return lax.dot_general(a, b, (((0,), (0,)), ((), ())), preferred_element_type=F32)


def _mod_kernel(c_ref, w_ref, b_ref, o_ref):
    c = c_ref[...]
    s = c * _sigmoid(c)
    o_ref[...] = jnp.dot(s.astype(BF16), w_ref[...].astype(BF16), preferred_element_type=F32) + b_ref[...]


def _modulation(c_all, w_mod, b_mod):
    n = N_MOD * D_MODEL
    tn = 1536
    return pl.pallas_call(
        _mod_kernel,
        grid=(n // tn,),
        in_specs=[pl.BlockSpec((16, D_MODEL), lambda j: (0, 0)),
                  pl.BlockSpec((D_MODEL, tn), lambda j: (0, j)),
                  pl.BlockSpec((1, tn), lambda j: (0, j))],
        out_specs=pl.BlockSpec((16, tn), lambda j: (0, j)),
        out_shape=jax.ShapeDtypeStruct((16, n), F32),
        compiler_params=_cparams(1),
        name="modulation",
    )(c_all, w_mod, b_mod.reshape(1, n))


def _modnorm(x, g, shift, scale):
    ms = jnp.mean(x * x, axis=-1, keepdims=True)
    return (x * lax.rsqrt(ms + NORM_EPS) * g) * (1.0 + scale) + shift


def _proj_kernel(x_ref, mod_ref, g_ref, w_ref, cos_ref, sin_ref, o_ref, h_ref):
    j = pl.program_id(1)

    @pl.when(j == 0)
    def _():
        h = _modnorm(x_ref[...], g_ref[...], mod_ref[0, 0:1, :], mod_ref[0, 1:2, :])
        h_ref[...] = h.astype(BF16)

    acc = jnp.dot(h_ref[...], w_ref[...], preferred_element_type=F32)

    @pl.when(j == ROPE_TILE)
    def _():
        width = acc.shape[1]
        lane = lax.broadcasted_iota(jnp.int32, acc.shape, 1)
        fwd = pltpu.roll(acc, width - 16, axis=1)
        bwd = pltpu.roll(acc, 16, axis=1)
        rot = jnp.where((lane % 32) < 16, fwd, bwd)
        o_ref[...] = acc * cos_ref[...] + rot * sin_ref[...]

    @pl.when(j != ROPE_TILE)
    def _():
        o_ref[...] = acc


def _projection(x_all, mod, g, w_all, cos_t, sin_t, n_lat, seq):
    n = x_all.shape[0]
    tm, tn = PROJ_TM, PROJ_TN
    n_lat_tiles = n_lat // tm
    tiles_per_seq = seq // tm
    n_batch = n_lat // seq

    def seg(i):
        return jnp.where(i < n_lat_tiles, i // tiles_per_seq, n_batch)

    def rope_blk(i):
        return jnp.where(i < n_lat_tiles, i % tiles_per_seq, tiles_per_seq)

    return pl.pallas_call(
        _proj_kernel,
        grid=(n // tm, P_COLS // tn),
        in_specs=[pl.BlockSpec((tm, D_MODEL), lambda i, j: (i, 0)),
                  pl.BlockSpec((1, N_MOD, D_MODEL), lambda i, j: (seg(i), 0, 0)),
                  pl.BlockSpec((1, D_MODEL), lambda i, j: (0, 0)),
                  pl.BlockSpec((D_MODEL, tn), lambda i, j: (0, j)),
                  pl.BlockSpec((tm, tn), lambda i, j: (rope_blk(i), 0)),
                  pl.BlockSpec((tm, tn), lambda i, j: (rope_blk(i), 0))],
        out_specs=pl.BlockSpec((tm, tn), lambda i, j: (i, j)),
        out_shape=jax.ShapeDtypeStruct((n, P_COLS), F32),
        scratch_shapes=[pltpu.VMEM((tm, D_MODEL), BF16)],
        compiler_params=_cparams(2),
        name="norm_projection",
    )(x_all, mod, g.reshape(1, D_MODEL), w_all, cos_t, sin_t)


def _rope_tables(seq):
    half = 16
    pos = jnp.arange(seq, dtype=jnp.int32)
    rows = (pos // GRID_W).astype(F32)
    cols = (pos % GRID_W).astype(F32)
    inv = jnp.power(ROPE_BASE, -jnp.arange(half, dtype=F32) / half)
    ar = rows[:, None] * inv[None, :]
    ac = cols[:, None] * inv[None, :]
    cos64 = jnp.concatenate([jnp.cos(ar), jnp.cos(ar), jnp.cos(ac), jnp.cos(ac)], axis=-1)
    sin64 = jnp.concatenate([-jnp.sin(ar), jnp.sin(ar), -jnp.sin(ac), jnp.sin(ac)], axis=-1)
    reps = 512 // 64
    cos512 = jnp.tile(cos64, (1, reps))
    sin512 = jnp.tile(sin64, (1, reps))
    qs = DA_HEAD_DIM ** -0.5
    cos_l = jnp.concatenate([cos512 * qs, cos512], axis=-1)
    sin_l = jnp.concatenate([sin512 * qs, sin512], axis=-1)
    cos_c = jnp.concatenate([jnp.full((PROJ_TM, 512), qs, F32), jnp.ones((PROJ_TM, 512), F32)], axis=-1)
    sin_c = jnp.zeros((PROJ_TM, 1024), F32)
    return jnp.concatenate([cos_l, cos_c], axis=0), jnp.concatenate([sin_l, sin_c], axis=0)


def _attn_kernel(lam_ref, g_ref, q_ref, kl_ref, vl_ref, kc_ref, vc_ref, o_ref, *, lam_init, n_lat_tiles):
    i = pl.program_id(2)

    @pl.when(i < n_lat_tiles)
    def _():
        _attend(lam_ref, g_ref, q_ref, (kl_ref, vl_ref, kc_ref, vc_ref), o_ref, lam_init)

    @pl.when(i >= n_lat_tiles)
    def _():
        _attend(lam_ref, g_ref, q_ref, (kc_ref, vc_ref), o_ref, lam_init)


def _attend(lam_ref, g_ref, q_ref, kv_refs, o_ref, lam_init):
    n_seg = len(kv_refs) // 2
    q = q_ref[...]
    lane = lax.broadcasted_iota(jnp.int32, q.shape, 1)
    q1 = jnp.where(lane < DA_HEAD_DIM, q, 0.0).astype(BF16)
    q2 = jnp.where(lane >= DA_HEAD_DIM, q, 0.0).astype(BF16)
    ks = [kv_refs[2 * s][...].astype(BF16) for s in range(n_seg)]
    vs = [kv_refs[2 * s + 1][...].astype(BF16) for s in range(n_seg)]

    def softmax_v(qm):
        ss = [_nt_dot(qm, k) for k in ks]
        m = jnp.max(ss[0], axis=-1, keepdims=True)
        for s in ss[1:]:
            m = jnp.maximum(m, jnp.max(s, axis=-1, keepdims=True))
        es = [jnp.exp(s - m) for s in ss]
        l = jnp.sum(es[0], axis=-1, keepdims=True)
        for e in es[1:]:
            l = l + jnp.sum(e, axis=-1, keepdims=True)
        o = jnp.dot(es[0].astype(BF16), vs[0], preferred_element_type=F32)
        for e, v in zip(es[1:], vs[1:]):
            o = o + jnp.dot(e.astype(BF16), v, preferred_element_type=F32)
        return o / l

    lv = lam_ref[...]
    lam = (jnp.exp(jnp.sum(lv[0:1] * lv[1:2], axis=-1, keepdims=True))
           - jnp.exp(jnp.sum(lv[2:3] * lv[3:4], axis=-1, keepdims=True)) + lam_init)
    o = softmax_v(q1) - lam * softmax_v(q2)
    ms = jnp.mean(o * o, axis=-1, keepdims=True)
    o_ref[...] = (o * lax.rsqrt(ms + DA_SUBLN_EPS) * g_ref[...]) * (1.0 - lam_init)


def _attention(p_all, da_lambda, subln_g, lam_init, n_lat, seq, ctx_len, with_ctx_queries):
    n = p_all.shape[0]
    n_batch = n_lat // seq
    hw = DA_V_DIM
    cq, ck, cv = C_Q // hw, C_K // hw, C_V // hw
    tq = ctx_len
    ctx_blk0 = n_lat // ctx_len
    qpb = seq // tq
    n_out = n if with_ctx_queries else n_lat

    def q_blk(b, i):
        return jnp.where(i < qpb, b * qpb + i, ctx_blk0 + b)

    return pl.pallas_call(
        functools.partial(_attn_kernel, lam_init=lam_init, n_lat_tiles=qpb),
        grid=(n_batch, DA_HEADS, qpb + (1 if with_ctx_queries else 0)),
        in_specs=[pl.BlockSpec((4, DA_HEAD_DIM), lambda b, h, i: (0, 0)),
                  pl.BlockSpec((1, hw), lambda b, h, i: (0, 0)),
                  pl.BlockSpec((tq, hw), lambda b, h, i: (q_blk(b, i), cq + h)),
                  pl.BlockSpec((seq, hw), lambda b, h, i: (b, ck + h)),
                  pl.BlockSpec((seq, hw), lambda b, h, i: (b, cv + h)),
                  pl.BlockSpec((ctx_len, hw), lambda b, h, i: (ctx_blk0 + b, ck + h)),
                  pl.BlockSpec((ctx_len, hw), lambda b, h, i: (ctx_blk0 + b, cv + h))],
        out_specs=pl.BlockSpec((tq, hw), lambda b, h, i: (q_blk(b, i), h)),
        out_shape=jax.ShapeDtypeStruct((n_out, DA_HEADS * hw), F32),
        compiler_params=_cparams(3),
        name="diff_attention",
    )(da_lambda, subln_g.reshape(1, hw), p_all, p_all, p_all, p_all, p_all)


def _conv_kernel(sb_ref, sg_ref, sx_ref, gp_ref, xp_ref, gn_ref, xn_ref, w_ref, o_ref, *, n_lat_tiles, tiles_per_seq):
    i = pl.program_id(0)
    starts = (i >= n_lat_tiles) | (i % tiles_per_seq == 0)
    ends = (i >= n_lat_tiles) | (i % tiles_per_seq == tiles_per_seq - 1)
    u = sg_ref[...] * sx_ref[...]
    length = u.shape[0]
    halo_prev = jnp.where(starts, 0.0, gp_ref[7:8, :] * xp_ref[7:8, :])
    halo_next = jnp.where(ends, 0.0, gn_ref[0:1, :] * xn_ref[0:1, :])
    t = lax.broadcasted_iota(jnp.int32, u.shape, 0)
    prev = jnp.where(t == 0, halo_prev, pltpu.roll(u, 1, axis=0))
    nxt = jnp.where(t == length - 1, halo_next, pltpu.roll(u, length - 1, axis=0))
    w = w_ref[...]
    o_ref[...] = sb_ref[...] * (w[0:1] * prev + w[1:2] * u + w[2:3] * nxt)


def _short_conv(p_all, conv_w, n_out, n_lat, seq, ctx_len):
    n = p_all.shape[0]
    tm = ctx_len
    tc = 256
    ncol = SC_WIDTH // tc
    cb, cg, cx = C_SB // tc, C_SG // tc, C_SX // tc
    sub = tm // 8
    last8 = n // 8 - 1

    def main(col):
        return pl.BlockSpec((tm, tc), lambda i, j: (i, col + j))

    def halo_prev(col):
        return pl.BlockSpec((8, tc), lambda i, j: (jnp.maximum(i * sub - 1, 0), col + j))

    def halo_next(col):
        return pl.BlockSpec((8, tc), lambda i, j: (jnp.minimum((i + 1) * sub, last8), col + j))

    return pl.pallas_call(
        functools.partial(_conv_kernel, n_lat_tiles=n_lat // tm, tiles_per_seq=seq // tm),
        grid=(n_out // tm, ncol),
        in_specs=[main(cb), main(cg), main(cx), halo_prev(cg), halo_prev(cx), halo_next(cg), halo_next(cx),
                  pl.BlockSpec((3, tc), lambda i, j: (0, j))],
        out_specs=pl.BlockSpec((tm, tc), lambda i, j: (i, j)),
        out_shape=jax.ShapeDtypeStruct((n_out, SC_WIDTH), F32),
        compiler_params=_cparams(2),
        name="short_conv",
    )(p_all, p_all, p_all, p_all, p_all, p_all, p_all, conv_w)


def _unit_triangular_inverse(lmat, reverse):
    n = lmat.shape[0]
    nt = n // 8
    ltiles = [lmat[j * 8:(j + 1) * 8, :] for j in range(nt)]
    row_i = lax.broadcasted_iota(jnp.int32, (8, n), 0)
    col_i = lax.broadcasted_iota(jnp.int32, (8, n), 1)
    ttiles = [(row_i + j * 8 == col_i).astype(F32) for j in range(nt)]
    order = range(0, n - 1) if not reverse else range(n - 1, 0, -1)
    for k in order:
        jt, jr = divmod(k, 8)
        row = ttiles[jt][jr:jr + 1, :]
        targets = range(jt, nt) if not reverse else range(0, jt + 1)
        for j in targets:
            ttiles[j] = ttiles[j] + ltiles[j][:, k:k + 1] * row
    return jnp.concatenate(ttiles, axis=0)


def _rwkv_kernel(rr_ref, rk_ref, rv_ref, lo_ref, w0_ref, w2_ref, a0_ref, a2_ref, kk_ref, ka_ref, rkp_ref,
                 y_ref, bon_ref, s_ref, *, direction, reverse):
    c = pl.program_id(1)

    @pl.when(c == 0)
    def _():
        s_ref[...] = jnp.zeros_like(s_ref)

    n = RW_CHUNK
    hd = RW_HEAD_DIM
    r = rr_ref[...]
    k = rk_ref[...]
    v = rv_ref[...]
    lo = lo_ref[...]
    dw = lo[:, direction * RW_LORA_W:(direction + 1) * RW_LORA_W]
    da = lo[:, (2 + direction) * RW_LORA_W:(3 + direction) * RW_LORA_W]
    wl = w0_ref[...] + jnp.dot(jnp.tanh(dw).astype(BF16), w2_ref[...].astype(BF16), preferred_element_type=F32)
    logw = -RW_DECAY_SCALE * _sigmoid(wl)
    alpha = _sigmoid(a0_ref[...] + jnp.dot(da.astype(BF16), a2_ref[...].astype(BF16), preferred_element_type=F32))
    keff = k * (1.0 + (alpha - 1.0) * ka_ref[...])
    kkraw = k * kk_ref[...]

    ti = lax.broadcasted_iota(jnp.int32, (n, n), 0)
    si = lax.broadcasted_iota(jnp.int32, (n, n), 1)
    before = (si > ti) if reverse else (si < ti)
    upto = (si >= ti) if reverse else (si <= ti)
    cl = jnp.dot(upto.astype(F32), logw, preferred_element_type=F32, precision=lax.Precision.HIGHEST)
    p_in = jnp.exp(cl)
    p_ex = jnp.exp(cl - logw)
    p_inv = jnp.exp(-cl)
    last = 0 if reverse else n - 1
    p_end = p_in[last:last + 1, :]
    bonus_w = r * keff * rkp_ref[...]

    ys, bons = [], []
    for h in range(RW_HEADS):
        hs = slice(h * hd, (h + 1) * hd)
        kkh = kkraw[:, hs]
        kkh = kkh * lax.rsqrt(jnp.maximum(jnp.sum(kkh * kkh, axis=-1, keepdims=True), 1e-24))
        vh = v[:, hs]
        a_t = -kkh * p_ex[:, hs]
        b_t = kkh * alpha[:, hs] * p_inv[:, hs]
        k_t = keff[:, hs] * p_inv[:, hs]
        r_t = r[:, hs] * p_in[:, hs]
        lhs = jnp.concatenate([a_t, r_t], axis=0).astype(BF16)
        rhs = jnp.concatenate([b_t, k_t], axis=0).astype(BF16)
        gram = _nt_dot(lhs, rhs)
        l_ab = jnp.where(before, gram[:n, :n], 0.0)
        l_ak = jnp.where(before, gram[:n, n:], 0.0)
        m_rb = jnp.where(upto, gram[n:, :n], 0.0)
        m_rk = jnp.where(upto, gram[n:, n:], 0.0)
        s0 = s_ref[h]
        on_state = _nt_dot(lhs, s0.astype(BF16))
        tinv = _unit_triangular_inverse(l_ab, reverse)
        u_rhs = on_state[:n] + jnp.dot(l_ak.astype(BF16), vh.astype(BF16), preferred_element_type=F32)
        u = jnp.dot(tinv.astype(BF16), u_rhs.astype(BF16), preferred_element_type=F32)
        uv = jnp.concatenate([u, vh], axis=0).astype(BF16)
        m_all = jnp.concatenate([m_rb, m_rk], axis=1).astype(BF16)
        ys.append(on_state[n:] + jnp.dot(m_all, uv, preferred_element_type=F32))
        pe = p_end[:, hs]
        bk = jnp.concatenate([b_t * pe, k_t * pe], axis=0).astype(BF16)
        s_ref[h] = s0 * pe + _tn_dot(uv, bk)
        bons.append(jnp.sum(bonus_w[:, hs], axis=-1, keepdims=True) * vh)
    y_ref[...] = jnp.concatenate(ys, axis=-1)
    bon_ref[...] = jnp.concatenate(bons, axis=-1)


def _rwkv_scan(p_all, p, direction, n_lat, seq, ctx_len):
    n = p_all.shape[0]
    n_batch = n_lat // seq
    ch = RW_CHUNK
    ctx_c = ctx_len // ch
    seq_c = seq // ch
    lat0 = n_lat // ch
    reverse = direction == 1
    cw = RW_WIDTH

    def row_blk(b, c):
        if reverse:
            return jnp.where(c < ctx_c, lat0 + b * ctx_c + (ctx_c - 1 - c), b * seq_c + (seq_c - 1 - (c - ctx_c)))
        return jnp.where(c < ctx_c, lat0 + b * ctx_c + c, b * seq_c + (c - ctx_c))

    def tok_spec(col):
        return pl.BlockSpec((ch, cw), lambda b, c: (row_blk(b, c), col // cw))

    def par_spec(rows):
        return pl.BlockSpec((rows, cw), lambda b, c: (0, 0))

    y, bon = pl.pallas_call(
        functools.partial(_rwkv_kernel, direction=direction, reverse=reverse),
        grid=(n_batch, ctx_c + seq_c),
        in_specs=[tok_spec(C_RR), tok_spec(C_RK), tok_spec(C_RV), tok_spec(C_LORA),
                  par_spec(1), par_spec(RW_LORA_W), par_spec(1), par_spec(RW_LORA_W),
                  par_spec(1), par_spec(1), par_spec(1)],
        out_specs=[pl.BlockSpec((ch, cw), lambda b, c: (row_blk(b, c), 0)),
                   pl.BlockSpec((ch, cw), lambda b, c: (row_blk(b, c), 0))],
        out_shape=[jax.ShapeDtypeStruct((n, cw), F32), jax.ShapeDtypeStruct((n, cw), F32)],
        scratch_shapes=[pltpu.VMEM((RW_HEADS, RW_HEAD_DIM, RW_HEAD_DIM), F32)],
        compiler_params=_cparams(2),
        name="rwkv_scan_rev" if reverse else "rwkv_scan_fwd",
    )(p_all, p_all, p_all, p_all,
      p['rw_w0'][direction].reshape(1, cw), p['rw_w2'][direction],
      p['rw_a0'][direction].reshape(1, cw), p['rw_a2'][direction],
      p['rw_k_k'].reshape(1, cw), p['rw_k_a'].reshape(1, cw), p['rw_r_k'].reshape(1, cw))
    return y, bon


def _split_dot(x, w_bf16):
    hi = x.astype(BF16)
    lo = (x - hi.astype(F32)).astype(BF16)
    return (jnp.dot(hi, w_bf16, preferred_element_type=F32) + jnp.dot(lo, w_bf16, preferred_element_type=F32))


def _merge_kernel(x_ref, mod_ref, gl_ref, oda_ref, osc_ref, y0_ref, y1_ref, b0_ref, b1_ref, lo_ref,
                  g2_ref, lng_ref, lnb_ref, gavg_ref, wb_ref, wo_ref, n2g_ref, rw_ref, rb_ref,
                  xo_ref, h2_ref, lg_ref):
    ysum = y0_ref[...] + y1_ref[...]
    gavg = gavg_ref[...]
    mu = _split_dot(ysum, gavg)
    yc = ysum - mu
    var = _split_dot(yc * yc, gavg)
    yn = yc * lax.rsqrt(var + RW_GN_EPS) * lng_ref[...] + lnb_ref[...]
    dg = lo_ref[:, 4 * RW_LORA_W:4 * RW_LORA_W + RW_GATE_PAD]
    gate = jnp.dot(_sigmoid(dg).astype(BF16), g2_ref[...], preferred_element_type=F32)
    o_rw = (yn + b0_ref[...] + b1_ref[...]) * gate

    branches = (oda_ref[...], osc_ref[...], o_rw)
    merged = None
    for i, br in enumerate(branches):
        proj = jnp.dot(br.astype(BF16), wb_ref[i], preferred_element_type=F32)
        term = _sigmoid(gl_ref[:, i * D_MODEL:(i + 1) * D_MODEL]) * proj
        merged = term if merged is None else merged + term
    y = jnp.dot(merged.astype(BF16), wo_ref[...], preferred_element_type=F32)
    x_new = x_ref[...] + mod_ref[0, 2:3, :] * y
    xo_ref[...] = x_new
    h2 = _modnorm(x_new, n2g_ref[...], mod_ref[0, 3:4, :], mod_ref[0, 4:5, :])
    for j in range(D_MODEL // 128):
        h2_ref[:, j, :] = h2[:, j * 128:(j + 1) * 128]
    lg_ref[...] = (jnp.dot(h2, rw_ref[...], preferred_element_type=F32, precision=lax.Precision.HIGHEST)
                   + rb_ref[...])


def _merge(x_all, mod, p_all, o_da, o_sc, y0, y1, bon0, bon1, p, n_out, n_lat, seq):
    n = n_out
    tm = 256
    n_lat_tiles = n_lat // tm
    tiles_per_seq = seq // tm
    n_batch = n_lat // seq

    def seg(i):
        return jnp.where(i < n_lat_tiles, i // tiles_per_seq, n_batch)

    def row(w):
        return pl.BlockSpec((tm, w), lambda i: (i, 0))

    def const(shape):
        nd = len(shape)
        return pl.BlockSpec(shape, lambda i: (0,) * nd)

    g2 = jnp.zeros((RW_GATE_PAD, RW_WIDTH), F32).at[:RW_GATE_LORA].set(p['rw_g2']).astype(BF16)
    head_of = jnp.arange(RW_WIDTH) // RW_HEAD_DIM
    gavg = (head_of[:, None] == head_of[None, :]).astype(F32) / RW_HEAD_DIM
    return pl.pallas_call(
        _merge_kernel,
        grid=(n // tm,),
        in_specs=[row(D_MODEL),
                  pl.BlockSpec((1, N_MOD, D_MODEL), lambda i: (seg(i), 0, 0)),
                  pl.BlockSpec((tm, 3 * D_MODEL), lambda i: (i, C_GL // (3 * D_MODEL))),
                  row(512), row(512), row(512), row(512), row(512), row(512),
                  pl.BlockSpec((tm, 512), lambda i: (i, C_LORA // 512)),
                  const((RW_GATE_PAD, RW_WIDTH)), const((1, RW_WIDTH)), const((1, RW_WIDTH)),
                  const((RW_WIDTH, RW_WIDTH)),
                  const((3, 512, D_MODEL)), const((D_MODEL, D_MODEL)), const((1, D_MODEL)),
                  const((D_MODEL, N_EXPERTS)), const((1, N_EXPERTS))],
        out_specs=[row(D_MODEL),
                   pl.BlockSpec((tm, 8, 128), lambda i: (i, 0, 0)),
                   row(N_EXPERTS)],
        out_shape=[jax.ShapeDtypeStruct((n, D_MODEL), F32),
                   jax.ShapeDtypeStruct((n, 8, 128), F32),
                   jax.ShapeDtypeStruct((n, N_EXPERTS), F32)],
        compiler_params=_cparams(1),
        name="merge_router",
    )(x_all, mod, p_all, o_da, o_sc, y0, y1, bon0, bon1, p_all,
      g2, p['rw_lnx_g'].reshape(1, RW_WIDTH), p['rw_lnx_b'].reshape(1, RW_WIDTH), gavg.astype(BF16),
      p['w_branch'].astype(BF16), p['w_out'].astype(BF16), p['norm2_g'].reshape(1, D_MODEL),
      p['router_w'], p['router_b'].reshape(1, N_EXPERTS))


def _row_gather_copy(src_hbm, idx, dst_buf, slot, r, sem):
    return pltpu.make_async_copy(src_hbm.at[idx], dst_buf.at[slot, r], sem.at[slot])


def _expert_kernel(be_ref, tok_ref, nvalid_ref, h2_hbm, w1_ref, b1_ref, w2_ref, b2_ref, gate_ref,
                   o_ref, buf, sem):
    i = pl.program_id(0)
    nvalid = nvalid_ref[0]
    mb = MOE_BLOCK

    def issue(blk, slot):
        def body(r, carry):
            _row_gather_copy(h2_hbm, tok_ref[blk * mb + r], buf, slot, r, sem).start()
            return carry
        lax.fori_loop(0, mb, body, 0)

    def wait_all(slot):
        def body(r, carry):
            _row_gather_copy(h2_hbm, 0, buf, slot, r, sem).wait()
            return carry
        lax.fori_loop(0, mb, body, 0)

    @pl.when((i == 0) & (nvalid > 0))
    def _():
        issue(0, 0)

    @pl.when(i + 1 < nvalid)
    def _():
        issue(i + 1, (i + 1) % 2)

    @pl.when(i < nvalid)
    def _():
        slot = i % 2
        wait_all(slot)
        xb = jnp.concatenate([buf[slot, :, j, :] for j in range(D_MODEL // 128)], axis=-1).astype(BF16)
        hid = jnp.dot(xb, w1_ref[0].astype(BF16), preferred_element_type=F32) + b1_ref[0]
        glu = jnp.minimum(hid[:, :D_FF], SWIGLU_LIMIT)
        lin = jnp.clip(hid[:, D_FF:], -SWIGLU_LIMIT, SWIGLU_LIMIT)
        act = glu * _sigmoid(SWIGLU_ALPHA * glu) * (lin + 1.0)
        y = jnp.dot(act.astype(BF16), w2_ref[0].astype(BF16), preferred_element_type=F32) + b2_ref[0]
        y = y * gate_ref[...]
        for j in range(D_MODEL // 128):
            o_ref[:, j, :] = y[:, j * 128:(j + 1) * 128]

    @pl.when(i >= nvalid)
    def _():
        o_ref[...] = jnp.zeros_like(o_ref)


def _expert_ffn(h2, block_expert, row_token, nvalid, row_gate, w1, b1, w2, b2):
    n_rows = row_token.shape[0]
    n_blocks = n_rows // MOE_BLOCK
    mb = MOE_BLOCK
    grid_spec = pltpu.PrefetchScalarGridSpec(
        num_scalar_prefetch=3,
        grid=(n_blocks,),
        in_specs=[pl.BlockSpec(memory_space=pl.ANY),
                  pl.BlockSpec((1, D_MODEL, 2 * D_FF), lambda i, be, tok, nv: (be[i], 0, 0)),
                  pl.BlockSpec((1, 1, 2 * D_FF), lambda i, be, tok, nv: (be[i], 0, 0)),
                  pl.BlockSpec((1, D_FF, D_MODEL), lambda i, be, tok, nv: (be[i], 0, 0)),
                  pl.BlockSpec((1, 1, D_MODEL), lambda i, be, tok, nv: (be[i], 0, 0)),
                  pl.BlockSpec((mb, 1), lambda i, be, tok, nv: (i, 0))],
        out_specs=pl.BlockSpec((mb, 8, 128), lambda i, be, tok, nv: (i, 0, 0)),
        scratch_shapes=[pltpu.VMEM((2, mb, 8, 128), F32), pltpu.SemaphoreType.DMA((2,))],
    )
    return pl.pallas_call(
        _expert_kernel,
        grid_spec=grid_spec,
        out_shape=jax.ShapeDtypeStruct((n_rows, 8, 128), F32),
        compiler_params=_cparams(1),
        name="expert_ffn",
    )(block_expert, row_token, nvalid, h2, w1, b1.reshape(N_EXPERTS, 1, 2 * D_FF), w2,
      b2.reshape(N_EXPERTS, 1, D_MODEL), row_gate.reshape(n_rows, 1))


def _combine_kernel(pos_ref, yb_hbm, x_ref, mod_ref, fg_ref, o_ref, buf, sem, *, tm, n_tiles, final_norm):
    i = pl.program_id(0)

    def copy(idx, slot, r):
        return pltpu.make_async_copy(yb_hbm.at[idx], buf.at[slot, r], sem.at[slot])

    def issue(tile, slot):
        def body(r, carry):
            copy(pos_ref[tile * (tm * TOP_K) + r], slot, r).start()
            return carry
        lax.fori_loop(0, tm * TOP_K, body, 0)

    @pl.when(i == 0)
    def _():
        issue(0, 0)

    @pl.when(i + 1 < n_tiles)
    def _():
        issue(i + 1, (i + 1) % 2)

    slot = i % 2

    def wait_body(r, carry):
        copy(0, slot, r).wait()
        return carry
    lax.fori_loop(0, tm * TOP_K, wait_body, 0)

    parts = []
    for j in range(D_MODEL // 128):
        acc = buf[slot, pl.ds(0, tm), j, :]
        for kk in range(1, TOP_K):
            acc = acc + buf[slot, pl.ds(kk * tm, tm), j, :]
        parts.append(acc)
    f = jnp.concatenate(parts, axis=-1)
    x_new = x_ref[...] + mod_ref[0, 5:6, :] * f
    if final_norm:
        ms = jnp.mean(x_new * x_new, axis=-1, keepdims=True)
        x_new = x_new * lax.rsqrt(ms + NORM_EPS) * fg_ref[...]
    o_ref[...] = x_new


def _combine(yb, pos, x_all, mod, final_g, n_out, n_lat, seq, final_norm):
    tm = 256
    n_tiles = n_out // tm
    n_lat_tiles = n_lat // tm
    tiles_per_seq = seq // tm
    n_batch = n_lat // seq

    def seg(i):
        return jnp.where(i < n_lat_tiles, i // tiles_per_seq, n_batch)

    grid_spec = pltpu.PrefetchScalarGridSpec(
        num_scalar_prefetch=1,
        grid=(n_tiles,),
        in_specs=[pl.BlockSpec(memory_space=pl.ANY),
                  pl.BlockSpec((tm, D_MODEL), lambda i, pos: (i, 0)),
                  pl.BlockSpec((1, N_MOD, D_MODEL), lambda i, pos: (seg(i), 0, 0)),
                  pl.BlockSpec((1, D_MODEL), lambda i, pos: (0, 0))],
        out_specs=pl.BlockSpec((tm, D_MODEL), lambda i, pos: (i, 0)),
        scratch_shapes=[pltpu.VMEM((2, tm * TOP_K, 8, 128), F32), pltpu.SemaphoreType.DMA((2,))],
    )
    return pl.pallas_call(
        functools.partial(_combine_kernel, tm=tm, n_tiles=n_tiles, final_norm=final_norm),
        grid_spec=grid_spec,
        out_shape=jax.ShapeDtypeStruct((n_out, D_MODEL), F32),
        compiler_params=_cparams(1),
        name="moe_combine",
    )(pos, yb, x_all, mod, final_g.reshape(1, D_MODEL))


def _routing(logits, tm):
    n_tok = logits.shape[0]
    top_val, top_idx = lax.top_k(logits, TOP_K)
    gates = jax.nn.softmax(top_val, axis=-1)
    n_assign = n_tok * TOP_K
    expert_of = top_idx.reshape(-1).astype(jnp.int32)
    token_of = jnp.arange(n_assign, dtype=jnp.int32) // TOP_K
    order = jnp.argsort(expert_of)
    e_sorted = expert_of[order]
    counts = jnp.bincount(expert_of, length=N_EXPERTS)
    padded = (counts + MOE_BLOCK - 1) // MOE_BLOCK * MOE_BLOCK
    start = jnp.cumsum(counts) - counts
    end_padded = jnp.cumsum(padded)
    start_padded = end_padded - padded
    dest = (start_padded[e_sorted] + jnp.arange(n_assign, dtype=jnp.int32) - start[e_sorted]).astype(jnp.int32)
    n_rows = -(-n_assign // MOE_BLOCK) * MOE_BLOCK + N_EXPERTS * MOE_BLOCK
    n_blocks = n_rows // MOE_BLOCK
    row_token = jnp.zeros((n_rows,), jnp.int32).at[dest].set(token_of[order])
    row_gate = jnp.zeros((n_rows,), F32).at[dest].set(gates.reshape(-1)[order])
    block_expert = jnp.minimum(
        jnp.searchsorted(end_padded, jnp.arange(n_blocks) * MOE_BLOCK, side='right'), N_EXPERTS - 1).astype(jnp.int32)
    nvalid = (end_padded[-1] // MOE_BLOCK).astype(jnp.int32).reshape(1)
    pos = jnp.zeros((n_assign,), jnp.int32).at[order].set(dest)
    pos = pos.reshape(n_tok // tm, tm, TOP_K).transpose(0, 2, 1).reshape(-1)
    return block_expert, row_token, nvalid, row_gate, pos


def _layer(x_all, mod, li, p, tables, final_g, n_lat, seq, ctx_len, last):
    cos_t, sin_t = tables
    lam_init = 0.8 - 0.6 * math.exp(-0.3 * li)
    pad = jnp.zeros((D_MODEL, P_COLS - C_LORA - 4 * RW_LORA_W - RW_GATE_LORA), F32)
    w_in = p['w_in']
    w_all = jnp.concatenate(
        [w_in[:, 4608:], w_in[:, :4608], p['rw_w1'][0], p['rw_w1'][1], p['rw_a1'][0], p['rw_a1'][1],
         p['rw_g1'], pad], axis=1).astype(BF16)
    p_all = _projection(x_all, mod, p['norm1_g'], w_all, cos_t, sin_t, n_lat, seq)
    o_da = _attention(p_all, p['da_lambda'], p['da_subln_g'], lam_init, n_lat, seq, ctx_len, not last)
    n_tok = n_lat if last else x_all.shape[0]
    o_sc = _short_conv(p_all, p['conv_w'], n_tok, n_lat, seq, ctx_len)
    y0, bon0 = _rwkv_scan(p_all, p, 0, n_lat, seq, ctx_len)
    y1, bon1 = _rwkv_scan(p_all, p, 1, n_lat, seq, ctx_len)
    x_mid, h2, logits = _merge(x_all, mod, p_all, o_da, o_sc, y0, y1, bon0, bon1, p, n_tok, n_lat, seq)
    block_expert, row_token, nvalid, row_gate, pos = _routing(logits, 256)
    yb = _expert_ffn(h2, block_expert, row_token, nvalid, row_gate,
                     p['exp_w1'], p['exp_b1'], p['exp_w2'], p['exp_b2'])
    return _combine(yb, pos, x_mid, mod, final_g, n_tok, n_lat, seq, last)


def kernel(x, c, ctx, c_ctx, w_mod, b_mod, norm1_g, norm2_g, w_in, da_lambda, da_subln_g, conv_w, rw_w0, rw_w1, rw_w2, rw_a0, rw_a1, rw_a2, rw_g1, rw_g2, rw_k_k, rw_k_a, rw_r_k, rw_lnx_g, rw_lnx_b, w_branch, w_out, router_w, router_b, exp_w1, exp_b1, exp_w2, exp_b2, final_g):
    n_batch, seq, d = x.shape
    ctx_len = ctx.shape[1]
    depth = w_in.shape[0]
    n_lat = n_batch * seq
    assert d == D_MODEL and n_batch <= 8
    assert seq % PROJ_TM == 0 and (n_batch * ctx_len) % PROJ_TM == 0 and seq % ctx_len == 0
    x_all = jnp.concatenate([x.reshape(n_lat, d), ctx.reshape(n_batch * ctx_len, d)], axis=0)
    c_all = jnp.zeros((16, d), F32).at[:n_batch].set(c).at[n_batch].set(c_ctx)
    tables = _rope_tables(seq)
    for li in range(depth):
        p = {
            'norm1_g': norm1_g[li], 'norm2_g': norm2_g[li], 'w_in': w_in[li], 'da_lambda': da_lambda[li],
            'da_subln_g': da_subln_g[li], 'conv_w': conv_w[li],
            'rw_w0': rw_w0[li], 'rw_w1': rw_w1[li], 'rw_w2': rw_w2[li],
            'rw_a0': rw_a0[li], 'rw_a1': rw_a1[li], 'rw_a2': rw_a2[li],
            'rw_g1': rw_g1[li], 'rw_g2': rw_g2[li], 'rw_k_k': rw_k_k[li], 'rw_k_a': rw_k_a[li],
            'rw_r_k': rw_r_k[li], 'rw_lnx_g': rw_lnx_g[li], 'rw_lnx_b': rw_lnx_b[li],
            'w_branch': w_branch[li], 'w_out': w_out[li], 'router_w': router_w[li], 'router_b': router_b[li],
            'exp_w1': exp_w1[li], 'exp_b1': exp_b1[li], 'exp_w2': exp_w2[li], 'exp_b2': exp_b2[li],
        }
        mod = _modulation(c_all, w_mod[li], b_mod[li]).reshape(16, N_MOD, D_MODEL)
        x_all = _layer(x_all, mod, li, p, tables, final_g, n_lat, seq, ctx_len, li == depth - 1)
    return x_all.reshape(n_batch, seq, d)
```

```python
import functools
import math

import jax
import jax.numpy as jnp
from jax import lax
from jax.experimental import pallas as pl
from jax.experimental.pallas import tpu as pltpu

F32 = jnp.float32
BF16 = jnp.bfloat16

D_MODEL = 1024
GRID_W = 64
NORM_EPS = 1e-6
N_MOD = 6

DA_HEADS = 4
DA_HEAD_DIM = 64
DA_V_DIM = 128
DA_SUBLN_EPS = 1e-5
ROPE_BASE = 10000.0

SC_WIDTH = 512

RW_HEADS = 8
RW_HEAD_DIM = 64
RW_WIDTH = 512
RW_DECAY_SCALE = 0.606531
RW_GN_EPS = 64e-5
RW_CHUNK = 64
RW_LORA_W = 64
RW_GATE_LORA = 160
RW_GATE_PAD = 256

N_EXPERTS = 32
TOP_K = 4
D_FF = 1024
SWIGLU_LIMIT = 7.0
SWIGLU_ALPHA = 1.702
MOE_BLOCK = 256

C_GL = 0
C_Q = 3072
C_K = 3584
C_V = 4096
C_SB = 4608
C_SG = 5120
C_SX = 5632
C_RR = 6144
C_RK = 6656
C_RV = 7168
C_LORA = 7680
P_COLS = 8192

PROJ_TM = 512
PROJ_TN = 1024
ROPE_TILE = C_Q // PROJ_TN

VMEM_LIMIT = 56 * 1024 * 1024


def _cparams(n_axes):
    return pltpu.CompilerParams(dimension_semantics=("arbitrary",) * n_axes, vmem_limit_bytes=VMEM_LIMIT)


def _sigmoid(x):
    return 1.0 / (1.0 + jnp.exp(-x))


def _nt_dot(a, b):
    return lax.dot_general(a, b, (((1,), (1,)), ((), ())), preferred_element_type=F32)


def _tn_dot(a, b):
    return lax.dot_general(a, b, (((0,), (0,)), ((), ())), preferred_element_type=F32)


def _mod_kernel(c_ref, w_ref, b_ref, o_ref):
    c = c_ref[...]
    s = c * _sigmoid(c)
    o_ref[...] = jnp.dot(s.astype(BF16), w_ref[...].astype(BF16), preferred_element_type=F32) + b_ref[...]


def _modulation(c_all, w_mod, b_mod):
    n = N_MOD * D_MODEL
    tn = 1536
    return pl.pallas_call(
        _mod_kernel,
        grid=(n // tn,),
        in_specs=[pl.BlockSpec((16, D_MODEL), lambda j: (0, 0)),
                  pl.BlockSpec((D_MODEL, tn), lambda j: (0, j)),
                  pl.BlockSpec((1, tn), lambda j: (0, j))],
        out_specs=pl.BlockSpec((16, tn), lambda j: (0, j)),
        out_shape=jax.ShapeDtypeStruct((16, n), F32),
        compiler_params=_cparams(1),
        name="modulation",
    )(c_all, w_mod, b_mod.reshape(1, n))


def _modnorm(x, g, shift, scale):
    ms = jnp.mean(x * x, axis=-1, keepdims=True)
    return (x * lax.rsqrt(ms + NORM_EPS) * g) * (1.0 + scale) + shift


def _proj_kernel(x_ref, mod_ref, g_ref, w_ref, cos_ref, sin_ref, o_ref, h_ref):
    j = pl.program_id(1)

    @pl.when(j == 0)
    def _():
        h = _modnorm(x_ref[...], g_ref[...], mod_ref[0, 0:1, :], mod_ref[0, 1:2, :])
        h_ref[...] = h.astype(BF16)

    acc = jnp.dot(h_ref[...], w_ref[...], preferred_element_type=F32)

    @pl.when(j == ROPE_TILE)
    def _():
        width = acc.shape[1]
        lane = lax.broadcasted_iota(jnp.int32, acc.shape, 1)
        fwd = pltpu.roll(acc, width - 16, axis=1)
        bwd = pltpu.roll(acc, 16, axis=1)
        rot = jnp.where((lane % 32) < 16, fwd, bwd)
        o_ref[...] = acc * cos_ref[...] + rot * sin_ref[...]

    @pl.when(j != ROPE_TILE)
    def _():
        o_ref[...] = acc


def _projection(x_all, mod, g, w_all, cos_t, sin_t, n_lat, seq):
    n = x_all.shape[0]
    tm, tn = PROJ_TM, PROJ_TN
    n_lat_tiles = n_lat // tm
    tiles_per_seq = seq // tm
    n_batch = n_lat // seq

    def seg(i):
        return jnp.where(i < n_lat_tiles, i // tiles_per_seq, n_batch)

    def rope_blk(i):
        return jnp.where(i < n_lat_tiles, i % tiles_per_seq, tiles_per_seq)

    return pl.pallas_call(
        _proj_kernel,
        grid=(n // tm, P_COLS // tn),
        in_specs=[pl.BlockSpec((tm, D_MODEL), lambda i, j: (i, 0)),
                  pl.BlockSpec((1, N_MOD, D_MODEL), lambda i, j: (seg(i), 0, 0)),
                  pl.BlockSpec((1, D_MODEL), lambda i, j: (0, 0)),
                  pl.BlockSpec((D_MODEL, tn), lambda i, j: (0, j)),
                  pl.BlockSpec((tm, tn), lambda i, j: (rope_blk(i), 0)),
                  pl.BlockSpec((tm, tn), lambda i, j: (rope_blk(i), 0))],
        out_specs=pl.BlockSpec((tm, tn), lambda i, j: (i, j)),
        out_shape=jax.ShapeDtypeStruct((n, P_COLS), F32),
        scratch_shapes=[pltpu.VMEM((tm, D_MODEL), BF16)],
        compiler_params=_cparams(2),
        name="norm_projection",
    )(x_all, mod, g.reshape(1, D_MODEL), w_all, cos_t, sin_t)


def _rope_tables(seq):
    half = 16
    pos = jnp.arange(seq, dtype=jnp.int32)
    rows = (pos // GRID_W).astype(F32)
    cols = (pos % GRID_W).astype(F32)
    inv = jnp.power(ROPE_BASE, -jnp.arange(half, dtype=F32) / half)
    ar = rows[:, None] * inv[None, :]
    ac = cols[:, None] * inv[None, :]
    cos64 = jnp.concatenate([jnp.cos(ar), jnp.cos(ar), jnp.cos(ac), jnp.cos(ac)], axis=-1)
    sin64 = jnp.concatenate([-jnp.sin(ar), jnp.sin(ar), -jnp.sin(ac), jnp.sin(ac)], axis=-1)
    reps = 512 // 64
    cos512 = jnp.tile(cos64, (1, reps))
    sin512 = jnp.tile(sin64, (1, reps))
    qs = DA_HEAD_DIM ** -0.5
    cos_l = jnp.concatenate([cos512 * qs, cos512], axis=-1)
    sin_l = jnp.concatenate([sin512 * qs, sin512], axis=-1)
    cos_c = jnp.concatenate([jnp.full((PROJ_TM, 512), qs, F32), jnp.ones((PROJ_TM, 512), F32)], axis=-1)
    sin_c = jnp.zeros((PROJ_TM, 1024), F32)
    return jnp.concatenate([cos_l, cos_c], axis=0), jnp.concatenate([sin_l, sin_c], axis=0)


def _attn_kernel(lam_ref, g_ref, q_ref, kl_ref, vl_ref, kc_ref, vc_ref, o_ref, *, lam_init, n_lat_tiles):
    i = pl.program_id(2)

    @pl.when(i < n_lat_tiles)
    def _():
        _attend(lam_ref, g_ref, q_ref, (kl_ref, vl_ref, kc_ref, vc_ref), o_ref, lam_init)

    @pl.when(i >= n_lat_tiles)
    def _():
        _attend(lam_ref, g_ref, q_ref, (kc_ref, vc_ref), o_ref, lam_init)


def _attend(lam_ref, g_ref, q_ref, kv_refs, o_ref, lam_init):
    n_seg = len(kv_refs) // 2
    q = q_ref[...]
    lane = lax.broadcasted_iota(jnp.int32, q.shape, 1)
    q1 = jnp.where(lane < DA_HEAD_DIM, q, 0.0).astype(BF16)
    q2 = jnp.where(lane >= DA_HEAD_DIM, q, 0.0).astype(BF16)
    ks = [kv_refs[2 * s][...].astype(BF16) for s in range(n_seg)]
    vs = [kv_refs[2 * s + 1][...].astype(BF16) for s in range(n_seg)]

    def softmax_v(qm):
        ss = [_nt_dot(qm, k) for k in ks]
        m = jnp.max(ss[0], axis=-1, keepdims=True)
        for s in ss[1:]:
            m = jnp.maximum(m, jnp.max(s, axis=-1, keepdims=True))
        es = [jnp.exp(s - m) for s in ss]
        l = jnp.sum(es[0], axis=-1, keepdims=True)
        for e in es[1:]:
            l = l + jnp.sum(e, axis=-1, keepdims=True)
        o = jnp.dot(es[0].astype(BF16), vs[0], preferred_element_type=F32)
        for e, v in zip(es[1:], vs[1:]):
            o = o + jnp.dot(e.astype(BF16), v, preferred_element_type=F32)
        return o / l

    lv = lam_ref[...]
    lam = (jnp.exp(jnp.sum(lv[0:1] * lv[1:2], axis=-1, keepdims=True))
           - jnp.exp(jnp.sum(lv[2:3] * lv[3:4], axis=-1, keepdims=True)) + lam_init)
    o = softmax_v(q1) - lam * softmax_v(q2)
    ms = jnp.mean(o * o, axis=-1, keepdims=True)
    o_ref[...] = (o * lax.rsqrt(ms + DA_SUBLN_EPS) * g_ref[...]) * (1.0 - lam_init)


def _attention(p_all, da_lambda, subln_g, lam_init, n_lat, seq, ctx_len, with_ctx_queries):
    n = p_all.shape[0]
    n_batch = n_lat // seq
    hw = DA_V_DIM
    cq, ck, cv = C_Q // hw, C_K // hw, C_V // hw
    tq = ctx_len
    ctx_blk0 = n_lat // ctx_len
    qpb = seq // tq
    n_out = n if with_ctx_queries else n_lat

    def q_blk(b, i):
        return jnp.where(i < qpb, b * qpb + i, ctx_blk0 + b)

    return pl.pallas_call(
        functools.partial(_attn_kernel, lam_init=lam_init, n_lat_tiles=qpb),
        grid=(n_batch, DA_HEADS, qpb + (1 if with_ctx_queries else 0)),
        in_specs=[pl.BlockSpec((4, DA_HEAD_DIM), lambda b, h, i: (0, 0)),
                  pl.BlockSpec((1, hw), lambda b, h, i: (0, 0)),
                  pl.BlockSpec((tq, hw), lambda b, h, i: (q_blk(b, i), cq + h)),
                  pl.BlockSpec((seq, hw), lambda b, h, i: (b, ck + h)),
                  pl.BlockSpec((seq, hw), lambda b, h, i: (b, cv + h)),
                  pl.BlockSpec((ctx_len, hw), lambda b, h, i: (ctx_blk0 + b, ck + h)),
                  pl.BlockSpec((ctx_len, hw), lambda b, h, i: (ctx_blk0 + b, cv + h))],
        out_specs=pl.BlockSpec((tq, hw), lambda b, h, i: (q_blk(b, i), h)),
        out_shape=jax.ShapeDtypeStruct((n_out, DA_HEADS * hw), F32),
        compiler_params=_cparams(3),
        name="diff_attention",
    )(da_lambda, subln_g.reshape(1, hw), p_all, p_all, p_all, p_all, p_all)


def _conv_kernel(sb_ref, sg_ref, sx_ref, gp_ref, xp_ref, gn_ref, xn_ref, w_ref, o_ref, *, n_lat_tiles, tiles_per_seq):
    i = pl.program_id(0)
    starts = (i >= n_lat_tiles) | (i % tiles_per_seq == 0)
    ends = (i >= n_lat_tiles) | (i % tiles_per_seq == tiles_per_seq - 1)
    u = sg_ref[...] * sx_ref[...]
    length = u.shape[0]
    halo_prev = jnp.where(starts, 0.0, gp_ref[7:8, :] * xp_ref[7:8, :])
    halo_next = jnp.where(ends, 0.0, gn_ref[0:1, :] * xn_ref[0:1, :])
    t = lax.broadcasted_iota(jnp.int32, u.shape, 0)
    prev = jnp.where(t == 0, halo_prev, pltpu.roll(u, 1, axis=0))
    nxt = jnp.where(t == length - 1, halo_next, pltpu.roll(u, length - 1, axis=0))
    w = w_ref[...]
    o_ref[...] = sb_ref[...] * (w[0:1] * prev + w[1:2] * u + w[2:3] * nxt)


def _short_conv(p_all, conv_w, n_out, n_lat, seq, ctx_len):
    n = p_all.shape[0]
    tm = ctx_len
    tc = 256
    ncol = SC_WIDTH // tc
    cb, cg, cx = C_SB // tc, C_SG // tc, C_SX // tc
    sub = tm // 8
    last8 = n // 8 - 1

    def main(col):
        return pl.BlockSpec((tm, tc), lambda i, j: (i, col + j))

    def halo_prev(col):
        return pl.BlockSpec((8, tc), lambda i, j: (jnp.maximum(i * sub - 1, 0), col + j))

    def halo_next(col):
        return pl.BlockSpec((8, tc), lambda i, j: (jnp.minimum((i + 1) * sub, last8), col + j))

    return pl.pallas_call(
        functools.partial(_conv_kernel, n_lat_tiles=n_lat // tm, tiles_per_seq=seq // tm),
        grid=(n_out // tm, ncol),
        in_specs=[main(cb), main(cg), main(cx), halo_prev(cg), halo_prev(cx), halo_next(cg), halo_next(cx),
                  pl.BlockSpec((3, tc), lambda i, j: (0, j))],
        out_specs=pl.BlockSpec((tm, tc), lambda i, j: (i, j)),
        out_shape=jax.ShapeDtypeStruct((n_out, SC_WIDTH), F32),
        compiler_params=_cparams(2),
        name="short_conv",
    )(p_all, p_all, p_all, p_all, p_all, p_all, p_all, conv_w)


def _bdot(a, b):
    return jnp.dot(a.astype(BF16), b.astype(BF16), preferred_element_type=F32)


def _block_unit_inverses(lmats, xor_idx):
    eye = jnp.where(xor_idx == 0, 1.0, 0.0)
    lds = [jnp.where(xor_idx < 8, l, 0.0).astype(BF16) for l in lmats]
    ld2s = [jnp.dot(ld, ld, preferred_element_type=F32).astype(BF16) for ld in lds]
    xs = [eye + ld.astype(F32) for ld in lds]
    xs = [x + jnp.dot(x.astype(BF16), ld2, preferred_element_type=F32) for x, ld2 in zip(xs, ld2s)]
    ld4s = [jnp.dot(ld2, ld2, preferred_element_type=F32).astype(BF16) for ld2 in ld2s]
    xs = [x + jnp.dot(x.astype(BF16), ld4, preferred_element_type=F32) for x, ld4 in zip(xs, ld4s)]
    b = 8
    while b < RW_CHUNK:
        offs = [jnp.where(xor_idx >= b, jnp.where(xor_idx < 2 * b, l, 0.0), 0.0).astype(BF16) for l in lmats]
        xbs = [x.astype(BF16) for x in xs]
        ts = [jnp.dot(xb, off, preferred_element_type=F32).astype(BF16) for xb, off in zip(xbs, offs)]
        xs = [x + jnp.dot(t, xb, preferred_element_type=F32) for x, t, xb in zip(xs, ts, xbs)]
        b *= 2
    return xs


def _rwkv_kernel(rr_ref, rk_ref, rv_ref, lo_ref, w0_ref, w2_ref, a0_ref, a2_ref, kk_ref, ka_ref, rkp_ref, gsum_ref,
                 y_ref, bon_ref, s_ref, *, direction, reverse):
    c = pl.program_id(1)

    @pl.when(c == 0)
    def _():
        s_ref[...] = jnp.zeros_like(s_ref)

    n = RW_CHUNK
    hd = RW_HEAD_DIM
    r = rr_ref[...]
    k = rk_ref[...]
    v = rv_ref[...]
    lo = lo_ref[...]
    dw = lo[:, direction * RW_LORA_W:(direction + 1) * RW_LORA_W]
    da = lo[:, (2 + direction) * RW_LORA_W:(3 + direction) * RW_LORA_W]
    wl = w0_ref[...] + jnp.dot(jnp.tanh(dw).astype(BF16), w2_ref[...].astype(BF16), preferred_element_type=F32)
    logw = -RW_DECAY_SCALE * _sigmoid(wl)
    alpha = _sigmoid(a0_ref[...] + jnp.dot(da.astype(BF16), a2_ref[...].astype(BF16), preferred_element_type=F32))
    keff = k * (1.0 + (alpha - 1.0) * ka_ref[...])
    kkraw = k * kk_ref[...]
    gsum = gsum_ref[...]
    kk = kkraw * lax.rsqrt(jnp.maximum(_split_dot(kkraw * kkraw, gsum), 1e-24))
    bon_ref[...] = _split_dot(r * keff * rkp_ref[...], gsum) * v

    ti = lax.broadcasted_iota(jnp.int32, (n, n), 0)
    si = lax.broadcasted_iota(jnp.int32, (n, n), 1)
    upto_n = (si >= ti) if reverse else (si <= ti)
    cl = jnp.dot(upto_n.astype(F32), logw, preferred_element_type=F32, precision=lax.Precision.HIGHEST)
    p_in = jnp.exp(cl)
    p_ex = jnp.exp(cl - logw)
    p_inv = jnp.exp(-cl)
    last = 0 if reverse else n - 1
    p_end = p_in[last:last + 1, :]
    a_t = -kk * p_ex
    b_t = kk * alpha * p_inv
    k_t = keff * p_inv
    r_t = r * p_in
    b_e = b_t * p_end
    k_e = k_t * p_end

    m2 = 2 * n
    t2 = lax.broadcasted_iota(jnp.int32, (m2, m2), 0)
    s2 = lax.broadcasted_iota(jnp.int32, (m2, m2), 1)
    xor_idx = t2 ^ s2
    same_head = xor_idx < n
    before = same_head & ((s2 > t2) if reverse else (s2 < t2))
    upto = same_head & ((s2 >= t2) if reverse else (s2 <= t2))
    first_head = lax.broadcasted_iota(jnp.int32, (n, m2), 1) < hd

    def stack(x):
        return jnp.concatenate([jnp.where(first_head, x, 0.0), jnp.where(first_head, 0.0, x)], axis=0)

    pairs = range(RW_HEADS // 2)
    sl = [slice(pr * m2, (pr + 1) * m2) for pr in pairs]
    vss = [stack(v[:, ps]) for ps in sl]
    lhss = [jnp.concatenate([stack(a_t[:, ps]), stack(r_t[:, ps])], axis=0).astype(BF16) for ps in sl]
    rhss = [jnp.concatenate([stack(b_t[:, ps]), stack(k_t[:, ps])], axis=0).astype(BF16) for ps in sl]
    grams = [_nt_dot(lhs, rhs) for lhs, rhs in zip(lhss, rhss)]
    s0s = [s_ref[pr] for pr in pairs]
    on_states = [_nt_dot(lhs, s0.astype(BF16)) for lhs, s0 in zip(lhss, s0s)]
    l_abs = [jnp.where(before, g[:m2, :m2], 0.0) for g in grams]
    l_aks = [jnp.where(before, g[:m2, m2:], 0.0).astype(BF16) for g in grams]
    m_alls = [jnp.concatenate([jnp.where(upto, g[m2:, :m2], 0.0), jnp.where(upto, g[m2:, m2:], 0.0)],
                              axis=1).astype(BF16) for g in grams]
    u_rhss = [os_[:m2] + jnp.dot(l_ak, vs.astype(BF16), preferred_element_type=F32)
              for os_, l_ak, vs in zip(on_states, l_aks, vss)]
    tinvs = _block_unit_inverses(l_abs, xor_idx)
    us = [_bdot(tinv, u_rhs) for tinv, u_rhs in zip(tinvs, u_rhss)]
    uvs = [jnp.concatenate([u, vs], axis=0).astype(BF16) for u, vs in zip(us, vss)]
    y2s = [os_[m2:] + jnp.dot(m_all, uv, preferred_element_type=F32) for os_, m_all, uv in zip(on_states, m_alls, uvs)]
    bks = [jnp.concatenate([stack(b_e[:, ps]), stack(k_e[:, ps])], axis=0).astype(BF16) for ps in sl]
    for pr in pairs:
        s_ref[pr] = s0s[pr] * p_end[:, sl[pr]] + _tn_dot(uvs[pr], bks[pr])
    y_ref[...] = jnp.concatenate([y2[:n] + y2[n:] for y2 in y2s], axis=-1)


def _rwkv_scan(p_all, p, direction, n_lat, seq, ctx_len):
    n = p_all.shape[0]
    n_batch = n_lat // seq
    ch = RW_CHUNK
    ctx_c = ctx_len // ch
    seq_c = seq // ch
    lat0 = n_lat // ch
    reverse = direction == 1
    cw = RW_WIDTH

    def row_blk(b, c):
        if reverse:
            return jnp.where(c < ctx_c, lat0 + b * ctx_c + (ctx_c - 1 - c), b * seq_c + (seq_c - 1 - (c - ctx_c)))
        return jnp.where(c < ctx_c, lat0 + b * ctx_c + c, b * seq_c + (c - ctx_c))

    def tok_spec(col):
        return pl.BlockSpec((ch, cw), lambda b, c: (row_blk(b, c), col // cw))

    def par_spec(rows):
        return pl.BlockSpec((rows, cw), lambda b, c: (0, 0))

    y, bon = pl.pallas_call(
        functools.partial(_rwkv_kernel, direction=direction, reverse=reverse),
        grid=(n_batch, ctx_c + seq_c),
        in_specs=[tok_spec(C_RR), tok_spec(C_RK), tok_spec(C_RV), tok_spec(C_LORA),
                  par_spec(1), par_spec(RW_LORA_W), par_spec(1), par_spec(RW_LORA_W),
                  par_spec(1), par_spec(1), par_spec(1), par_spec(cw)],
        out_specs=[pl.BlockSpec((ch, cw), lambda b, c: (row_blk(b, c), 0)),
                   pl.BlockSpec((ch, cw), lambda b, c: (row_blk(b, c), 0))],
        out_shape=[jax.ShapeDtypeStruct((n, cw), F32), jax.ShapeDtypeStruct((n, cw), F32)],
        scratch_shapes=[pltpu.VMEM((RW_HEADS // 2, 2 * RW_HEAD_DIM, 2 * RW_HEAD_DIM), F32)],
        compiler_params=_cparams(2),
        name="rwkv_scan_rev" if reverse else "rwkv_scan_fwd",
    )(p_all, p_all, p_all, p_all,
      p['rw_w0'][direction].reshape(1, cw), p['rw_w2'][direction],
      p['rw_a0'][direction].reshape(1, cw), p['rw_a2'][direction],
      p['rw_k_k'].reshape(1, cw), p['rw_k_a'].reshape(1, cw), p['rw_r_k'].reshape(1, cw), _head_group_ones())
    return y, bon


def _head_group_ones():
    head_of = jnp.arange(RW_WIDTH) // RW_HEAD_DIM
    return (head_of[:, None] == head_of[None, :]).astype(BF16)


def _split_dot(x, w_bf16):
    hi = x.astype(BF16)
    lo = (x - hi.astype(F32)).astype(BF16)
    return (jnp.dot(hi, w_bf16, preferred_element_type=F32) + jnp.dot(lo, w_bf16, preferred_element_type=F32))


def _merge_kernel(x_ref, mod_ref, gl_ref, oda_ref, osc_ref, y0_ref, y1_ref, b0_ref, b1_ref, lo_ref,
                  g2_ref, lng_ref, lnb_ref, gavg_ref, wb_ref, wo_ref, n2g_ref, rw_ref, rb_ref,
                  xo_ref, h2_ref, route_ref, cnt_ref):
    @pl.when(pl.program_id(0) == 0)
    def _():
        cnt_ref[...] = jnp.zeros_like(cnt_ref)

    ysum = y0_ref[...] + y1_ref[...]
    gavg = gavg_ref[...]
    mu = _split_dot(ysum, gavg)
    yc = ysum - mu
    var = _split_dot(yc * yc, gavg)
    yn = yc * lax.rsqrt(var + RW_GN_EPS) * lng_ref[...] + lnb_ref[...]
    dg = lo_ref[:, 4 * RW_LORA_W:4 * RW_LORA_W + RW_GATE_PAD]
    gate = jnp.dot(_sigmoid(dg).astype(BF16), g2_ref[...], preferred_element_type=F32)
    o_rw = (yn + b0_ref[...] + b1_ref[...]) * gate

    branches = (oda_ref[...], osc_ref[...], o_rw)
    merged = None
    for i, br in enumerate(branches):
        proj = jnp.dot(br.astype(BF16), wb_ref[i], preferred_element_type=F32)
        term = _sigmoid(gl_ref[:, i * D_MODEL:(i + 1) * D_MODEL]) * proj
        merged = term if merged is None else merged + term
    y = jnp.dot(merged.astype(BF16), wo_ref[...], preferred_element_type=F32)
    x_new = x_ref[...] + mod_ref[0, 2:3, :] * y
    xo_ref[...] = x_new
    h2 = _modnorm(x_new, n2g_ref[...], mod_ref[0, 3:4, :], mod_ref[0, 4:5, :])
    h2_ref[...] = h2
    logits = (jnp.dot(h2, rw_ref[...], preferred_element_type=F32, precision=lax.Precision.HIGHEST)
              + rb_ref[...])
    _route(logits, route_ref, cnt_ref)


def _route(logits, route_ref, cnt_ref):
    tm, ne = logits.shape
    lane = lax.broadcasted_iota(jnp.int32, logits.shape, 1)
    cur = logits
    vals, onehots, ids = [], [], []
    for _ in range(TOP_K):
        m = jnp.max(cur, axis=-1, keepdims=True)
        idx = jnp.min(jnp.where(cur == m, lane, ne), axis=-1, keepdims=True)
        hit = lane == idx
        vals.append(m)
        ids.append(idx.astype(F32))
        onehots.append(jnp.where(hit, 1.0, 0.0))
        cur = jnp.where(hit, -jnp.inf, cur)
    es = [jnp.exp(v - vals[0]) for v in vals]
    denom = es[0]
    for e in es[1:]:
        denom = denom + e
    gates = [e / denom for e in es]
    ti = lax.broadcasted_iota(jnp.int32, (tm, tm), 0)
    si = lax.broadcasted_iota(jnp.int32, (tm, tm), 1)
    earlier = jnp.where(si < ti, 1.0, 0.0).astype(BF16)
    base = cnt_ref[...]
    ranks = []
    for oh in onehots:
        prefix = jnp.dot(earlier, oh.astype(BF16), preferred_element_type=F32)
        ranks.append(jnp.sum(oh * (base + prefix), axis=-1, keepdims=True))
        base = base + jnp.sum(oh, axis=0, keepdims=True)
    cnt_ref[...] = base
    out_lane = lax.broadcasted_iota(jnp.int32, route_ref.shape, 1)
    out = jnp.zeros(route_ref.shape, F32)
    for j, col in enumerate(ids + ranks + gates):
        out = jnp.where(out_lane == j, col, out)
    route_ref[...] = out


def _merge(x_all, mod, p_all, o_da, o_sc, y0, y1, bon0, bon1, p, n_out, n_lat, seq):
    n = n_out
    tm = 256
    n_lat_tiles = n_lat // tm
    tiles_per_seq = seq // tm
    n_batch = n_lat // seq

    def seg(i):
        return jnp.where(i < n_lat_tiles, i // tiles_per_seq, n_batch)

    def row(w):
        return pl.BlockSpec((tm, w), lambda i: (i, 0))

    def const(shape):
        nd = len(shape)
        return pl.BlockSpec(shape, lambda i: (0,) * nd)

    g2 = jnp.zeros((RW_GATE_PAD, RW_WIDTH), F32).at[:RW_GATE_LORA].set(p['rw_g2']).astype(BF16)
    head_of = jnp.arange(RW_WIDTH) // RW_HEAD_DIM
    gavg = (head_of[:, None] == head_of[None, :]).astype(F32) / RW_HEAD_DIM
    return pl.pallas_call(
        _merge_kernel,
        grid=(n // tm,),
        in_specs=[row(D_MODEL),
                  pl.BlockSpec((1, N_MOD, D_MODEL), lambda i: (seg(i), 0, 0)),
                  pl.BlockSpec((tm, 3 * D_MODEL), lambda i: (i, C_GL // (3 * D_MODEL))),
                  row(512), row(512), row(512), row(512), row(512), row(512),
                  pl.BlockSpec((tm, 512), lambda i: (i, C_LORA // 512)),
                  const((RW_GATE_PAD, RW_WIDTH)), const((1, RW_WIDTH)), const((1, RW_WIDTH)),
                  const((RW_WIDTH, RW_WIDTH)),
                  const((3, 512, D_MODEL)), const((D_MODEL, D_MODEL)), const((1, D_MODEL)),
                  const((D_MODEL, N_EXPERTS)), const((1, N_EXPERTS))],
        out_specs=[row(D_MODEL),
                   row(D_MODEL),
                   row(128),
                   const((1, N_EXPERTS))],
        out_shape=[jax.ShapeDtypeStruct((n, D_MODEL), F32),
                   jax.ShapeDtypeStruct((n, D_MODEL), F32),
                   jax.ShapeDtypeStruct((n, 128), F32),
                   jax.ShapeDtypeStruct((1, N_EXPERTS), F32)],
        compiler_params=_cparams(1),
        name="merge_router",
    )(x_all, mod, p_all, o_da, o_sc, y0, y1, bon0, bon1, p_all,
      g2, p['rw_lnx_g'].reshape(1, RW_WIDTH), p['rw_lnx_b'].reshape(1, RW_WIDTH), gavg.astype(BF16),
      p['w_branch'].astype(BF16), p['w_out'].astype(BF16), p['norm2_g'].reshape(1, D_MODEL),
      p['router_w'], p['router_b'].reshape(1, N_EXPERTS))


def _dispatch_kernel(pos_ref, h2_ref, xb_in, xb_out, sem, *, tm):
    del xb_in
    i = pl.program_id(0)

    def copy(t, row):
        return pltpu.make_async_copy(h2_ref.at[pl.ds(t, 1), :], xb_out.at[pl.ds(row, 1), :], sem.at[0])

    def issue(t, carry):
        base = (i * tm + t) * TOP_K
        for k in range(TOP_K):
            copy(t, pos_ref[base + k]).start()
        return carry

    def drain(t, carry):
        for k in range(TOP_K):
            copy(t, 0).wait()
        return carry

    lax.fori_loop(0, tm, issue, 0)
    lax.fori_loop(0, tm, drain, 0)


def _dispatch(h2, pos, n_rows):
    n_tok = pos.shape[0] // TOP_K
    tm = 256
    grid_spec = pltpu.PrefetchScalarGridSpec(
        num_scalar_prefetch=1,
        grid=(n_tok // tm,),
        in_specs=[pl.BlockSpec((tm, D_MODEL), lambda i, pos: (i, 0)),
                  pl.BlockSpec(memory_space=pl.ANY)],
        out_specs=pl.BlockSpec(memory_space=pl.ANY),
        scratch_shapes=[pltpu.SemaphoreType.DMA((1,))],
    )
    return pl.pallas_call(
        functools.partial(_dispatch_kernel, tm=tm),
        grid_spec=grid_spec,
        out_shape=jax.ShapeDtypeStruct((n_rows, D_MODEL), F32),
        input_output_aliases={2: 0},
        compiler_params=pltpu.CompilerParams(dimension_semantics=("arbitrary",), vmem_limit_bytes=VMEM_LIMIT,
                                             disable_bounds_checks=True),
        name="moe_dispatch",
    )(pos, h2, jnp.zeros((n_rows, D_MODEL), F32))


def _expert_kernel(be_ref, nvalid_ref, x_ref, w1_ref, b1_ref, w2_ref, b2_ref, o_ref, w1b, w2b):
    i = pl.program_id(0)
    nvalid = nvalid_ref[0]

    @pl.when(i < nvalid)
    def _():
        @pl.when((i == 0) | (be_ref[i] != be_ref[jnp.maximum(i - 1, 0)]))
        def _():
            w1b[...] = w1_ref[0, 0].astype(BF16)
            w2b[...] = w2_ref[0, 0].astype(BF16)

        hid = jnp.dot(x_ref[...].astype(BF16), w1b[...], preferred_element_type=F32) + b1_ref[0, 0]
        glu = jnp.minimum(hid[:, :D_FF], SWIGLU_LIMIT)
        lin = jnp.clip(hid[:, D_FF:], -SWIGLU_LIMIT, SWIGLU_LIMIT)
        act = glu * _sigmoid(SWIGLU_ALPHA * glu) * (lin + 1.0)
        o_ref[...] = jnp.dot(act.astype(BF16), w2b[...], preferred_element_type=F32) + b2_ref[0, 0]

    @pl.when(i >= nvalid)
    def _():
        o_ref[...] = jnp.zeros_like(o_ref)


def _expert_ffn(xb, block_expert, nvalid, li, w1, b1, w2, b2):
    n_rows = xb.shape[0]
    mb = MOE_BLOCK
    depth = w1.shape[0]
    grid_spec = pltpu.PrefetchScalarGridSpec(
        num_scalar_prefetch=2,
        grid=(n_rows // mb,),
        in_specs=[pl.BlockSpec((mb, D_MODEL), lambda i, be, nv: (i, 0)),
                  pl.BlockSpec((1, 1, D_MODEL, 2 * D_FF), lambda i, be, nv: (li, be[i], 0, 0)),
                  pl.BlockSpec((1, 1, 1, 2 * D_FF), lambda i, be, nv: (li, be[i], 0, 0)),
                  pl.BlockSpec((1, 1, D_FF, D_MODEL), lambda i, be, nv: (li, be[i], 0, 0)),
                  pl.BlockSpec((1, 1, 1, D_MODEL), lambda i, be, nv: (li, be[i], 0, 0))],
        out_specs=pl.BlockSpec((mb, D_MODEL), lambda i, be, nv: (i, 0)),
        scratch_shapes=[pltpu.VMEM((D_MODEL, 2 * D_FF), BF16), pltpu.VMEM((D_FF, D_MODEL), BF16)],
    )
    return pl.pallas_call(
        _expert_kernel,
        grid_spec=grid_spec,
        out_shape=jax.ShapeDtypeStruct((n_rows, D_MODEL), F32),
        compiler_params=_cparams(1),
        name="expert_ffn",
    )(block_expert, nvalid, xb, w1, b1.reshape(depth, N_EXPERTS, 1, 2 * D_FF), w2,
      b2.reshape(depth, N_EXPERTS, 1, D_MODEL))


def _combine_kernel(pos_ref, yb_hbm, x_ref, route_ref, mod_ref, fg_ref, o_ref, buf, sem, *, tm, n_tiles, final_norm):
    i = pl.program_id(0)

    def copy(idx, slot, r):
        return pltpu.make_async_copy(yb_hbm.at[pl.ds(idx, 1), :], buf.at[slot, pl.ds(r, 1), :], sem.at[slot])

    def issue(tile, slot):
        def body(r, carry):
            copy(pos_ref[tile * (tm * TOP_K) + r], slot, r).start()
            return carry
        lax.fori_loop(0, tm * TOP_K, body, 0)

    @pl.when(i == 0)
    def _():
        issue(0, 0)

    @pl.when(i + 1 < n_tiles)
    def _():
        issue(i + 1, (i + 1) % 2)

    slot = i % 2

    def wait_body(r, carry):
        copy(0, slot, r).wait()
        return carry
    lax.fori_loop(0, tm * TOP_K, wait_body, 0)

    f = route_ref[:, 2 * TOP_K:2 * TOP_K + 1] * buf[slot, pl.ds(0, tm), :]
    for kk in range(1, TOP_K):
        f = f + route_ref[:, 2 * TOP_K + kk:2 * TOP_K + kk + 1] * buf[slot, pl.ds(kk * tm, tm), :]
    x_new = x_ref[...] + mod_ref[0, 5:6, :] * f
    if final_norm:
        ms = jnp.mean(x_new * x_new, axis=-1, keepdims=True)
        x_new = x_new * lax.rsqrt(ms + NORM_EPS) * fg_ref[...]
    o_ref[...] = x_new


def _combine(yb, pos, x_all, route, mod, final_g, n_out, n_lat, seq, final_norm):
    tm = 256
    n_tiles = n_out // tm
    n_lat_tiles = n_lat // tm
    tiles_per_seq = seq // tm
    n_batch = n_lat // seq

    def seg(i):
        return jnp.where(i < n_lat_tiles, i // tiles_per_seq, n_batch)

    grid_spec = pltpu.PrefetchScalarGridSpec(
        num_scalar_prefetch=1,
        grid=(n_tiles,),
        in_specs=[pl.BlockSpec(memory_space=pl.ANY),
                  pl.BlockSpec((tm, D_MODEL), lambda i, pos: (i, 0)),
                  pl.BlockSpec((tm, 128), lambda i, pos: (i, 0)),
                  pl.BlockSpec((1, N_MOD, D_MODEL), lambda i, pos: (seg(i), 0, 0)),
                  pl.BlockSpec((1, D_MODEL), lambda i, pos: (0, 0))],
        out_specs=pl.BlockSpec((tm, D_MODEL), lambda i, pos: (i, 0)),
        scratch_shapes=[pltpu.VMEM((2, tm * TOP_K, D_MODEL), F32), pltpu.SemaphoreType.DMA((2,))],
    )
    return pl.pallas_call(
        functools.partial(_combine_kernel, tm=tm, n_tiles=n_tiles, final_norm=final_norm),
        grid_spec=grid_spec,
        out_shape=jax.ShapeDtypeStruct((n_out, D_MODEL), F32),
        compiler_params=pltpu.CompilerParams(dimension_semantics=("arbitrary",), vmem_limit_bytes=VMEM_LIMIT,
                                             disable_bounds_checks=True),
        name="moe_combine",
    )(pos, yb, x_all, route, mod, final_g.reshape(1, D_MODEL))


def _routing(route, counts):
    n_tok = route.shape[0]
    n_assign = n_tok * TOP_K
    counts = counts.reshape(N_EXPERTS).astype(jnp.int32)
    padded = (counts + MOE_BLOCK - 1) // MOE_BLOCK * MOE_BLOCK
    end_padded = jnp.cumsum(padded)
    start_padded = end_padded - padded
    n_rows = -(-n_assign // MOE_BLOCK) * MOE_BLOCK + N_EXPERTS * MOE_BLOCK
    n_blocks = n_rows // MOE_BLOCK
    block_expert = jnp.minimum(
        jnp.searchsorted(end_padded, jnp.arange(n_blocks) * MOE_BLOCK, side='right'), N_EXPERTS - 1).astype(jnp.int32)
    nvalid = (end_padded[-1] // MOE_BLOCK).astype(jnp.int32).reshape(1)
    expert = route[:, :TOP_K].astype(jnp.int32)
    rank = route[:, TOP_K:2 * TOP_K].astype(jnp.int32)
    onehot = expert[:, :, None] == jnp.arange(N_EXPERTS, dtype=jnp.int32)[None, None, :]
    pos = rank + jnp.sum(jnp.where(onehot, start_padded[None, None, :], 0), axis=-1)
    return block_expert, nvalid, pos.reshape(-1).astype(jnp.int32), n_rows


def _layer(x_all, mod, li, p, tables, final_g, n_lat, seq, ctx_len, last):
    cos_t, sin_t = tables
    lam_init = 0.8 - 0.6 * math.exp(-0.3 * li)
    pad = jnp.zeros((D_MODEL, P_COLS - C_LORA - 4 * RW_LORA_W - RW_GATE_LORA), F32)
    w_in = p['w_in']
    w_all = jnp.concatenate(
        [w_in[:, 4608:], w_in[:, :4608], p['rw_w1'][0], p['rw_w1'][1], p['rw_a1'][0], p['rw_a1'][1],
         p['rw_g1'], pad], axis=1).astype(BF16)
    p_all = _projection(x_all, mod, p['norm1_g'], w_all, cos_t, sin_t, n_lat, seq)
    o_da = _attention(p_all, p['da_lambda'], p['da_subln_g'], lam_init, n_lat, seq, ctx_len, not last)
    n_tok = n_lat if last else x_all.shape[0]
    o_sc = _short_conv(p_all, p['conv_w'], n_tok, n_lat, seq, ctx_len)
    y0, bon0 = _rwkv_scan(p_all, p, 0, n_lat, seq, ctx_len)
    y1, bon1 = _rwkv_scan(p_all, p, 1, n_lat, seq, ctx_len)
    x_mid, h2, route, counts = _merge(x_all, mod, p_all, o_da, o_sc, y0, y1, bon0, bon1, p, n_tok, n_lat, seq)
    block_expert, nvalid, pos, n_rows = _routing(route, counts)
    xb = _dispatch(h2, pos, n_rows)
    yb = _expert_ffn(xb, block_expert, nvalid, li, p['exp_w1'], p['exp_b1'], p['exp_w2'], p['exp_b2'])
    tm = 256
    pos_by_slot = pos.reshape(n_tok // tm, tm, TOP_K).transpose(0, 2, 1).reshape(-1)
    return _combine(yb, pos_by_slot, x_mid, route, mod, final_g, n_tok, n_lat, seq, last)


def kernel(x, c, ctx, c_ctx, w_mod, b_mod, norm1_g, norm2_g, w_in, da_lambda, da_subln_g, conv_w, rw_w0, rw_w1, rw_w2, rw_a0, rw_a1, rw_a2, rw_g1, rw_g2, rw_k_k, rw_k_a, rw_r_k, rw_lnx_g, rw_lnx_b, w_branch, w_out, router_w, router_b, exp_w1, exp_b1, exp_w2, exp_b2, final_g):
    n_batch, seq, d = x.shape
    ctx_len = ctx.shape[1]
    depth = w_in.shape[0]
    n_lat = n_batch * seq
    assert d == D_MODEL and n_batch <= 8
    assert seq % PROJ_TM == 0 and (n_batch * ctx_len) % PROJ_TM == 0 and seq % ctx_len == 0
    x_all = jnp.concatenate([x.reshape(n_lat, d), ctx.reshape(n_batch * ctx_len, d)], axis=0)
    c_all = jnp.zeros((16, d), F32).at[:n_batch].set(c).at[n_batch].set(c_ctx)
    tables = _rope_tables(seq)
    for li in range(depth):
        p = {
            'norm1_g': norm1_g[li], 'norm2_g': norm2_g[li], 'w_in': w_in[li], 'da_lambda': da_lambda[li],
            'da_subln_g': da_subln_g[li], 'conv_w': conv_w[li],
            'rw_w0': rw_w0[li], 'rw_w1': rw_w1[li], 'rw_w2': rw_w2[li],
            'rw_a0': rw_a0[li], 'rw_a1': rw_a1[li], 'rw_a2': rw_a2[li],
            'rw_g1': rw_g1[li], 'rw_g2': rw_g2[li], 'rw_k_k': rw_k_k[li], 'rw_k_a': rw_k_a[li],
            'rw_r_k': rw_r_k[li], 'rw_lnx_g': rw_lnx_g[li], 'rw_lnx_b': rw_lnx_b[li],
            'w_branch': w_branch[li], 'w_out': w_out[li], 'router_w': router_w[li], 'router_b': router_b[li],
            'exp_w1': exp_w1, 'exp_b1': exp_b1, 'exp_w2': exp_w2, 'exp_b2': exp_b2,
        }
        mod = _modulation(c_all, w_mod[li], b_mod[li]).reshape(16, N_MOD, D_MODEL)
        x_all = _layer(x_all, mod, li, p, tables, final_g, n_lat, seq, ctx_len, li == depth - 1)
    return x_all.reshape(n_batch, seq, d)
```

```python
import functools
import math

import jax
import jax.numpy as jnp
from jax import lax
from jax.experimental import pallas as pl
from jax.experimental.pallas import tpu as pltpu

F32 = jnp.float32
BF16 = jnp.bfloat16

D_MODEL = 1024
GRID_W = 64
NORM_EPS = 1e-6
N_MOD = 6

DA_HEADS = 4
DA_HEAD_DIM = 64
DA_V_DIM = 128
DA_SUBLN_EPS = 1e-5
ROPE_BASE = 10000.0

SC_WIDTH = 512

RW_HEADS = 8
RW_HEAD_DIM = 64
RW_WIDTH = 512
RW_DECAY_SCALE = 0.606531
RW_GN_EPS = 64e-5
RW_CHUNK = 64
RW_LORA_W = 64
RW_GATE_LORA = 160
RW_GATE_PAD = 256

N_EXPERTS = 32
TOP_K = 4
D_FF = 1024
SWIGLU_LIMIT = 7.0
SWIGLU_ALPHA = 1.702
MOE_BLOCK = 512

C_GL = 0
C_Q = 3072
C_K = 3584
C_V = 4096
C_SB = 4608
C_SG = 5120
C_SX = 5632
C_RR = 6144
C_RK = 6656
C_RV = 7168
C_LORA = 7680
P_COLS = 8192

PROJ_TMS = (512, 1024)
PROJ_TN = 1024
ROPE_TILE = C_Q // PROJ_TN

VMEM_LIMIT = 56 * 1024 * 1024
DMA_UNROLL = 8


def _cparams(n_axes):
    return pltpu.CompilerParams(dimension_semantics=("arbitrary",) * n_axes, vmem_limit_bytes=VMEM_LIMIT)


def _sigmoid(x):
    return 1.0 / (1.0 + jnp.exp(-x))


def _nt_dot(a, b):
    return lax.dot_general(a, b, (((1,), (1,)), ((), ())), preferred_element_type=F32)


def _tn_dot(a, b):
    return lax.dot_general(a, b, (((0,), (0,)), ((), ())), preferred_element_type=F32)


def _mod_kernel(c_ref, w_ref, b_ref, o_ref):
    c = c_ref[...]
    s = c * _sigmoid(c)
    o_ref[...] = jnp.dot(s.astype(BF16), w_ref[...].astype(BF16), preferred_element_type=F32) + b_ref[...]


def _modulation(c_all, w_mod, b_mod):
    n = N_MOD * D_MODEL
    tn = 1536
    return pl.pallas_call(
        _mod_kernel,
        grid=(n // tn,),
        in_specs=[pl.BlockSpec((16, D_MODEL), lambda j: (0, 0)),
                  pl.BlockSpec((D_MODEL, tn), lambda j: (0, j)),
                  pl.BlockSpec((1, tn), lambda j: (0, j))],
        out_specs=pl.BlockSpec((16, tn), lambda j: (0, j)),
        out_shape=jax.ShapeDtypeStruct((16, n), F32),
        compiler_params=_cparams(1),
        name="modulation",
    )(c_all, w_mod, b_mod.reshape(1, n))


def _modnorm(x, g, shift, scale):
    ms = jnp.mean(x * x, axis=-1, keepdims=True)
    return (x * lax.rsqrt(ms + NORM_EPS) * g) * (1.0 + scale) + shift


def _proj_kernel(x_ref, mod_ref, g_ref, w_ref, cos_ref, sin_ref, o_ref, h_ref):
    j = pl.program_id(1)

    @pl.when(j == 0)
    def _():
        h = _modnorm(x_ref[...], g_ref[...], mod_ref[0, 0:1, :], mod_ref[0, 1:2, :])
        h_ref[...] = h.astype(BF16)

    acc = jnp.dot(h_ref[...], w_ref[...], preferred_element_type=F32)

    @pl.when(j == ROPE_TILE)
    def _():
        width = acc.shape[1]
        lane = lax.broadcasted_iota(jnp.int32, acc.shape, 1)
        fwd = pltpu.roll(acc, width - 16, axis=1)
        bwd = pltpu.roll(acc, 16, axis=1)
        rot = jnp.where((lane % 32) < 16, fwd, bwd)
        o_ref[...] = acc * cos_ref[...] + rot * sin_ref[...]

    @pl.when(j != ROPE_TILE)
    def _():
        o_ref[...] = acc


def _projection(x_all, mod, g, w_all, cos_t, sin_t, n_lat, seq):
    n = x_all.shape[0]
    tm, tn = cos_t.shape[0] - seq, PROJ_TN
    n_lat_tiles = n_lat // tm
    tiles_per_seq = seq // tm
    n_batch = n_lat // seq

    def seg(i):
        return jnp.where(i < n_lat_tiles, i // tiles_per_seq, n_batch)

    def rope_blk(i):
        return jnp.where(i < n_lat_tiles, i % tiles_per_seq, tiles_per_seq)

    return pl.pallas_call(
        _proj_kernel,
        grid=(n // tm, P_COLS // tn),
        in_specs=[pl.BlockSpec((tm, D_MODEL), lambda i, j: (i, 0)),
                  pl.BlockSpec((1, N_MOD, D_MODEL), lambda i, j: (seg(i), 0, 0)),
                  pl.BlockSpec((1, D_MODEL), lambda i, j: (0, 0)),
                  pl.BlockSpec((D_MODEL, tn), lambda i, j: (0, j)),
                  pl.BlockSpec((tm, tn), lambda i, j: (rope_blk(i), 0)),
                  pl.BlockSpec((tm, tn), lambda i, j: (rope_blk(i), 0))],
        out_specs=pl.BlockSpec((tm, tn), lambda i, j: (i, j)),
        out_shape=jax.ShapeDtypeStruct((n, P_COLS), F32),
        scratch_shapes=[pltpu.VMEM((tm, D_MODEL), BF16)],
        compiler_params=_cparams(2),
        name="norm_projection",
    )(x_all, mod, g.reshape(1, D_MODEL), w_all, cos_t, sin_t)


def _rope_tables(seq, tm):
    half = 16
    pos = jnp.arange(seq, dtype=jnp.int32)
    rows = (pos // GRID_W).astype(F32)
    cols = (pos % GRID_W).astype(F32)
    inv = jnp.power(ROPE_BASE, -jnp.arange(half, dtype=F32) / half)
    ar = rows[:, None] * inv[None, :]
    ac = cols[:, None] * inv[None, :]
    cos64 = jnp.concatenate([jnp.cos(ar), jnp.cos(ar), jnp.cos(ac), jnp.cos(ac)], axis=-1)
    sin64 = jnp.concatenate([-jnp.sin(ar), jnp.sin(ar), -jnp.sin(ac), jnp.sin(ac)], axis=-1)
    reps = 512 // 64
    cos512 = jnp.tile(cos64, (1, reps))
    sin512 = jnp.tile(sin64, (1, reps))
    qs = DA_HEAD_DIM ** -0.5
    cos_l = jnp.concatenate([cos512 * qs, cos512], axis=-1)
    sin_l = jnp.concatenate([sin512 * qs, sin512], axis=-1)
    cos_c = jnp.concatenate([jnp.full((tm, 512), qs, F32), jnp.ones((tm, 512), F32)], axis=-1)
    sin_c = jnp.zeros((tm, 1024), F32)
    return jnp.concatenate([cos_l, cos_c], axis=0), jnp.concatenate([sin_l, sin_c], axis=0)


def _attn_kernel(lam_ref, g_ref, q_ref, kl_ref, vl_ref, kc_ref, vc_ref, o_ref, *, lam_init, n_lat_tiles):
    i = pl.program_id(2)

    @pl.when(i < n_lat_tiles)
    def _():
        _attend(lam_ref, g_ref, q_ref, (kl_ref, vl_ref, kc_ref, vc_ref), o_ref, lam_init)

    @pl.when(i >= n_lat_tiles)
    def _():
        _attend(lam_ref, g_ref, q_ref, (kc_ref, vc_ref), o_ref, lam_init)


def _attend(lam_ref, g_ref, q_ref, kv_refs, o_ref, lam_init):
    n_seg = len(kv_refs) // 2
    q = q_ref[...]
    lane = lax.broadcasted_iota(jnp.int32, q.shape, 1)
    q1 = jnp.where(lane < DA_HEAD_DIM, q, 0.0).astype(BF16)
    q2 = jnp.where(lane >= DA_HEAD_DIM, q, 0.0).astype(BF16)
    ks = [kv_refs[2 * s][...].astype(BF16) for s in range(n_seg)]
    vs = [kv_refs[2 * s + 1][...].astype(BF16) for s in range(n_seg)]

    def softmax_v(qm):
        ss = [_nt_dot(qm, k) for k in ks]
        m = jnp.max(ss[0], axis=-1, keepdims=True)
        for s in ss[1:]:
            m = jnp.maximum(m, jnp.max(s, axis=-1, keepdims=True))
        es = [jnp.exp(s - m) for s in ss]
        l = jnp.sum(es[0], axis=-1, keepdims=True)
        for e in es[1:]:
            l = l + jnp.sum(e, axis=-1, keepdims=True)
        o = jnp.dot(es[0].astype(BF16), vs[0], preferred_element_type=F32)
        for e, v in zip(es[1:], vs[1:]):
            o = o + jnp.dot(e.astype(BF16), v, preferred_element_type=F32)
        return o / l

    lv = lam_ref[...]
    lam = (jnp.exp(jnp.sum(lv[0:1] * lv[1:2], axis=-1, keepdims=True))
           - jnp.exp(jnp.sum(lv[2:3] * lv[3:4], axis=-1, keepdims=True)) + lam_init)
    o = softmax_v(q1) - lam * softmax_v(q2)
    ms = jnp.mean(o * o, axis=-1, keepdims=True)
    o_ref[...] = (o * lax.rsqrt(ms + DA_SUBLN_EPS) * g_ref[...]) * (1.0 - lam_init)


def _attention(p_all, da_lambda, subln_g, lam_init, n_lat, seq, ctx_len, with_ctx_queries):
    n = p_all.shape[0]
    n_batch = n_lat // seq
    hw = DA_V_DIM
    cq, ck, cv = C_Q // hw, C_K // hw, C_V // hw
    tq = ctx_len
    ctx_blk0 = n_lat // ctx_len
    qpb = seq // tq
    n_out = n if with_ctx_queries else n_lat

    def q_blk(b, i):
        return jnp.where(i < qpb, b * qpb + i, ctx_blk0 + b)

    return pl.pallas_call(
        functools.partial(_attn_kernel, lam_init=lam_init, n_lat_tiles=qpb),
        grid=(n_batch, DA_HEADS, qpb + (1 if with_ctx_queries else 0)),
        in_specs=[pl.BlockSpec((4, DA_HEAD_DIM), lambda b, h, i: (0, 0)),
                  pl.BlockSpec((1, hw), lambda b, h, i: (0, 0)),
                  pl.BlockSpec((tq, hw), lambda b, h, i: (q_blk(b, i), cq + h)),
                  pl.BlockSpec((seq, hw), lambda b, h, i: (b, ck + h)),
                  pl.BlockSpec((seq, hw), lambda b, h, i: (b, cv + h)),
                  pl.BlockSpec((ctx_len, hw), lambda b, h, i: (ctx_blk0 + b, ck + h)),
                  pl.BlockSpec((ctx_len, hw), lambda b, h, i: (ctx_blk0 + b, cv + h))],
        out_specs=pl.BlockSpec((tq, hw), lambda b, h, i: (q_blk(b, i), h)),
        out_shape=jax.ShapeDtypeStruct((n_out, DA_HEADS * hw), F32),
        compiler_params=_cparams(3),
        name="diff_attention",
    )(da_lambda, subln_g.reshape(1, hw), p_all, p_all, p_all, p_all, p_all)


def _conv_kernel(sb_ref, sg_ref, sx_ref, gp_ref, xp_ref, gn_ref, xn_ref, w_ref, o_ref, *, n_lat_tiles, tiles_per_seq):
    i = pl.program_id(0)
    starts = (i >= n_lat_tiles) | (i % tiles_per_seq == 0)
    ends = (i >= n_lat_tiles) | (i % tiles_per_seq == tiles_per_seq - 1)
    u = sg_ref[...] * sx_ref[...]
    length = u.shape[0]
    halo_prev = jnp.where(starts, 0.0, gp_ref[7:8, :] * xp_ref[7:8, :])
    halo_next = jnp.where(ends, 0.0, gn_ref[0:1, :] * xn_ref[0:1, :])
    t = lax.broadcasted_iota(jnp.int32, u.shape, 0)
    prev = jnp.where(t == 0, halo_prev, pltpu.roll(u, 1, axis=0))
    nxt = jnp.where(t == length - 1, halo_next, pltpu.roll(u, length - 1, axis=0))
    w = w_ref[...]
    o_ref[...] = sb_ref[...] * (w[0:1] * prev + w[1:2] * u + w[2:3] * nxt)


def _short_conv(p_all, conv_w, n_out, n_lat, seq, ctx_len):
    n = p_all.shape[0]
    tm = ctx_len
    tc = 256
    ncol = SC_WIDTH // tc
    cb, cg, cx = C_SB // tc, C_SG // tc, C_SX // tc
    sub = tm // 8
    last8 = n // 8 - 1

    def main(col):
        return pl.BlockSpec((tm, tc), lambda i, j: (i, col + j))

    def halo_prev(col):
        return pl.BlockSpec((8, tc), lambda i, j: (jnp.maximum(i * sub - 1, 0), col + j))

    def halo_next(col):
        return pl.BlockSpec((8, tc), lambda i, j: (jnp.minimum((i + 1) * sub, last8), col + j))

    return pl.pallas_call(
        functools.partial(_conv_kernel, n_lat_tiles=n_lat // tm, tiles_per_seq=seq // tm),
        grid=(n_out // tm, ncol),
        in_specs=[main(cb), main(cg), main(cx), halo_prev(cg), halo_prev(cx), halo_next(cg), halo_next(cx),
                  pl.BlockSpec((3, tc), lambda i, j: (0, j))],
        out_specs=pl.BlockSpec((tm, tc), lambda i, j: (i, j)),
        out_shape=jax.ShapeDtypeStruct((n_out, SC_WIDTH), F32),
        compiler_params=_cparams(2),
        name="short_conv",
    )(p_all, p_all, p_all, p_all, p_all, p_all, p_all, conv_w)


def _bdot(a, b):
    return jnp.dot(a.astype(BF16), b.astype(BF16), preferred_element_type=F32)


def _block_unit_inverses(lmats, xor_idx):
    eye = jnp.where(xor_idx == 0, 1.0, 0.0)
    lds = [jnp.where(xor_idx < 8, l, 0.0).astype(BF16) for l in lmats]
    ld2s = [jnp.dot(ld, ld, preferred_element_type=F32).astype(BF16) for ld in lds]
    xs = [eye + ld.astype(F32) for ld in lds]
    xs = [x + jnp.dot(x.astype(BF16), ld2, preferred_element_type=F32) for x, ld2 in zip(xs, ld2s)]
    ld4s = [jnp.dot(ld2, ld2, preferred_element_type=F32).astype(BF16) for ld2 in ld2s]
    xs = [x + jnp.dot(x.astype(BF16), ld4, preferred_element_type=F32) for x, ld4 in zip(xs, ld4s)]
    b = 8
    while b < RW_CHUNK:
        offs = [jnp.where(xor_idx >= b, jnp.where(xor_idx < 2 * b, l, 0.0), 0.0).astype(BF16) for l in lmats]
        xbs = [x.astype(BF16) for x in xs]
        ts = [jnp.dot(xb, off, preferred_element_type=F32).astype(BF16) for xb, off in zip(xbs, offs)]
        xs = [x + jnp.dot(t, xb, preferred_element_type=F32) for x, t, xb in zip(xs, ts, xbs)]
        b *= 2
    return xs


def _rwkv_prepare(r, k, v, lo, w0, w2, a0, a2, k_k, k_a, r_k, gsum, direction):
    n = RW_CHUNK
    reverse = direction == 1
    dw = lo[:, direction * RW_LORA_W:(direction + 1) * RW_LORA_W]
    da = lo[:, (2 + direction) * RW_LORA_W:(3 + direction) * RW_LORA_W]
    wl = w0 + jnp.dot(jnp.tanh(dw).astype(BF16), w2.astype(BF16), preferred_element_type=F32)
    logw = -RW_DECAY_SCALE * _sigmoid(wl)
    alpha = _sigmoid(a0 + jnp.dot(da.astype(BF16), a2.astype(BF16), preferred_element_type=F32))
    keff = k * (1.0 + (alpha - 1.0) * k_a)
    kkraw = k * k_k
    kk = kkraw * lax.rsqrt(jnp.maximum(_split_dot(kkraw * kkraw, gsum), 1e-24))
    bonus = _split_dot(r * keff * r_k, gsum) * v
    ti = lax.broadcasted_iota(jnp.int32, (n, n), 0)
    si = lax.broadcasted_iota(jnp.int32, (n, n), 1)
    upto_n = (si >= ti) if reverse else (si <= ti)
    cl = jnp.dot(upto_n.astype(F32), logw, preferred_element_type=F32, precision=lax.Precision.HIGHEST)
    p_in = jnp.exp(cl)
    p_inv = jnp.exp(-cl)
    last = 0 if reverse else n - 1
    p_end = p_in[last:last + 1, :]
    b_t = kk * alpha * p_inv
    k_t = keff * p_inv
    return dict(v=v, a=-kk * jnp.exp(cl - logw), b=b_t, k=k_t, r=r * p_in, b_end=b_t * p_end, k_end=k_t * p_end,
                p_end=p_end, bonus=bonus)


def _rwkv_kernel(rr0_ref, rk0_ref, rv0_ref, lo0_ref, rr1_ref, rk1_ref, rv1_ref, lo1_ref,
                 w0_ref, w2_ref, a0_ref, a2_ref, kk_ref, ka_ref, rkp_ref, gsum_ref,
                 y0_ref, bon0_ref, y1_ref, bon1_ref, s_ref):
    c = pl.program_id(1)

    @pl.when(c == 0)
    def _():
        s_ref[...] = jnp.zeros_like(s_ref)

    n = RW_CHUNK
    hd = RW_HEAD_DIM
    tok_refs = ((rr0_ref, rk0_ref, rv0_ref, lo0_ref), (rr1_ref, rk1_ref, rv1_ref, lo1_ref))
    out_refs = ((y0_ref, bon0_ref), (y1_ref, bon1_ref))
    gsum = gsum_ref[...]
    prep = []
    for d in range(2):
        rr, rk, rv, lo = tok_refs[d]
        prep.append(_rwkv_prepare(rr[...], rk[...], rv[...], lo[...], w0_ref[d], w2_ref[d], a0_ref[d], a2_ref[d],
                                  kk_ref[...], ka_ref[...], rkp_ref[...], gsum, d))
        out_refs[d][1][...] = prep[d]['bonus']

    m2 = 2 * n
    t2 = lax.broadcasted_iota(jnp.int32, (m2, m2), 0)
    s2 = lax.broadcasted_iota(jnp.int32, (m2, m2), 1)
    xor_idx = t2 ^ s2
    same_head = xor_idx < n
    before = (same_head & (s2 < t2), same_head & (s2 > t2))
    upto = (same_head & (s2 <= t2), same_head & (s2 >= t2))
    first_head = lax.broadcasted_iota(jnp.int32, (n, m2), 1) < hd

    def stack(x):
        return jnp.concatenate([jnp.where(first_head, x, 0.0), jnp.where(first_head, 0.0, x)], axis=0)

    n_pairs = RW_HEADS // 2
    units = [(d, pr) for d in range(2) for pr in range(n_pairs)]

    def part(name, d, pr):
        return prep[d][name][:, pr * m2:(pr + 1) * m2]

    vss = [stack(part('v', d, pr)) for d, pr in units]
    lhss = [jnp.concatenate([stack(part('a', d, pr)), stack(part('r', d, pr))], axis=0).astype(BF16)
            for d, pr in units]
    rhss = [jnp.concatenate([stack(part('b', d, pr)), stack(part('k', d, pr))], axis=0).astype(BF16)
            for d, pr in units]
    grams = [_nt_dot(lhs, rhs) for lhs, rhs in zip(lhss, rhss)]
    s0s = [s_ref[d, pr] for d, pr in units]
    on_states = [_nt_dot(lhs, s0.astype(BF16)) for lhs, s0 in zip(lhss, s0s)]
    l_abs = [jnp.where(before[d], g[:m2, :m2], 0.0) for (d, _), g in zip(units, grams)]
    l_aks = [jnp.where(before[d], g[:m2, m2:], 0.0).astype(BF16) for (d, _), g in zip(units, grams)]
    m_alls = [jnp.concatenate([jnp.where(upto[d], g[m2:, :m2], 0.0), jnp.where(upto[d], g[m2:, m2:], 0.0)],
                              axis=1).astype(BF16) for (d, _), g in zip(units, grams)]
    u_rhss = [os_[:m2] + jnp.dot(l_ak, vs.astype(BF16), preferred_element_type=F32)
              for os_, l_ak, vs in zip(on_states, l_aks, vss)]
    tinvs = _block_unit_inverses(l_abs, xor_idx)
    us = [_bdot(tinv, u_rhs) for tinv, u_rhs in zip(tinvs, u_rhss)]
    uvs = [jnp.concatenate([u, vs], axis=0).astype(BF16) for u, vs in zip(us, vss)]
    y2s = [os_[m2:] + jnp.dot(m_all, uv, preferred_element_type=F32) for os_, m_all, uv in zip(on_states, m_alls, uvs)]
    bks = [jnp.concatenate([stack(part('b_end', d, pr)), stack(part('k_end', d, pr))], axis=0).astype(BF16)
           for d, pr in units]
    for i, (d, pr) in enumerate(units):
        s_ref[d, pr] = s0s[i] * part('p_end', d, pr) + _tn_dot(uvs[i], bks[i])
    for d in range(2):
        out_refs[d][0][...] = jnp.concatenate(
            [y2s[d * n_pairs + pr][:n] + y2s[d * n_pairs + pr][n:] for pr in range(n_pairs)], axis=-1)


def _rwkv_scan(p_all, p, n_lat, seq, ctx_len):
    n = p_all.shape[0]
    n_batch = n_lat // seq
    ch = RW_CHUNK
    ctx_c = ctx_len // ch
    seq_c = seq // ch
    lat0 = n_lat // ch
    cw = RW_WIDTH

    def blk_fwd(b, c):
        return jnp.where(c < ctx_c, lat0 + b * ctx_c + c, b * seq_c + (c - ctx_c))

    def blk_bwd(b, c):
        return jnp.where(c < ctx_c, lat0 + b * ctx_c + (ctx_c - 1 - c), b * seq_c + (seq_c - 1 - (c - ctx_c)))

    def tok_specs(blk):
        return [pl.BlockSpec((ch, cw), lambda b, c, col=col: (blk(b, c), col // cw))
                for col in (C_RR, C_RK, C_RV, C_LORA)]

    def par_spec(shape):
        nd = len(shape)
        return pl.BlockSpec(shape, lambda b, c: (0,) * nd)

    def out_specs(blk):
        return [pl.BlockSpec((ch, cw), lambda b, c: (blk(b, c), 0))] * 2

    sds = jax.ShapeDtypeStruct((n, cw), F32)
    return pl.pallas_call(
        _rwkv_kernel,
        grid=(n_batch, ctx_c + seq_c),
        in_specs=tok_specs(blk_fwd) + tok_specs(blk_bwd)
        + [par_spec((2, 1, cw)), par_spec((2, RW_LORA_W, cw)), par_spec((2, 1, cw)), par_spec((2, RW_LORA_W, cw)),
           par_spec((1, cw)), par_spec((1, cw)), par_spec((1, cw)), par_spec((cw, cw))],
        out_specs=out_specs(blk_fwd) + out_specs(blk_bwd),
        out_shape=[sds, sds, sds, sds],
        scratch_shapes=[pltpu.VMEM((2, RW_HEADS // 2, 2 * RW_HEAD_DIM, 2 * RW_HEAD_DIM), F32)],
        compiler_params=_cparams(2),
        name="rwkv_scan",
    )(*([p_all] * 8),
      p['rw_w0'].reshape(2, 1, cw), p['rw_w2'], p['rw_a0'].reshape(2, 1, cw), p['rw_a2'],
      p['rw_k_k'].reshape(1, cw), p['rw_k_a'].reshape(1, cw), p['rw_r_k'].reshape(1, cw), _head_group_ones())


def _head_group_ones():
    head_of = jnp.arange(RW_WIDTH) // RW_HEAD_DIM
    return (head_of[:, None] == head_of[None, :]).astype(BF16)


def _split_dot(x, w_bf16):
    hi = x.astype(BF16)
    lo = (x - hi.astype(F32)).astype(BF16)
    return (jnp.dot(hi, w_bf16, preferred_element_type=F32) + jnp.dot(lo, w_bf16, preferred_element_type=F32))


def _merge_kernel(x_ref, mod_ref, gl_ref, oda_ref, osc_ref, y0_ref, y1_ref, b0_ref, b1_ref, lo_ref,
                  g2_ref, lng_ref, lnb_ref, gavg_ref, wb_ref, wo_ref, n2g_ref, rw_ref, rb_ref,
                  xo_ref, h2_ref, route_ref, cnt_ref):
    @pl.when(pl.program_id(0) == 0)
    def _():
        cnt_ref[...] = jnp.zeros_like(cnt_ref)

    ysum = y0_ref[...] + y1_ref[...]
    gavg = gavg_ref[...]
    mu = _split_dot(ysum, gavg)
    yc = ysum - mu
    var = _split_dot(yc * yc, gavg)
    yn = yc * lax.rsqrt(var + RW_GN_EPS) * lng_ref[...] + lnb_ref[...]
    dg = lo_ref[:, 4 * RW_LORA_W:4 * RW_LORA_W + RW_GATE_PAD]
    gate = jnp.dot(_sigmoid(dg).astype(BF16), g2_ref[...], preferred_element_type=F32)
    o_rw = (yn + b0_ref[...] + b1_ref[...]) * gate

    branches = (oda_ref[...], osc_ref[...], o_rw)
    merged = None
    for i, br in enumerate(branches):
        proj = jnp.dot(br.astype(BF16), wb_ref[i], preferred_element_type=F32)
        term = _sigmoid(gl_ref[:, i * D_MODEL:(i + 1) * D_MODEL]) * proj
        merged = term if merged is None else merged + term
    y = jnp.dot(merged.astype(BF16), wo_ref[...], preferred_element_type=F32)
    x_new = x_ref[...] + mod_ref[0, 2:3, :] * y
    xo_ref[...] = x_new
    h2 = _modnorm(x_new, n2g_ref[...], mod_ref[0, 3:4, :], mod_ref[0, 4:5, :])
    h2_ref[...] = h2
    logits = (jnp.dot(h2, rw_ref[...], preferred_element_type=F32, precision=lax.Precision.HIGHEST)
              + rb_ref[...])
    _route(logits, route_ref, cnt_ref)


def _route(logits, route_ref, cnt_ref):
    tm, ne = logits.shape
    lane = lax.broadcasted_iota(jnp.int32, logits.shape, 1)
    cur = logits
    vals, onehots, ids = [], [], []
    for _ in range(TOP_K):
        m = jnp.max(cur, axis=-1, keepdims=True)
        idx = jnp.min(jnp.where(cur == m, lane, ne), axis=-1, keepdims=True)
        hit = lane == idx
        vals.append(m)
        ids.append(idx.astype(F32))
        onehots.append(jnp.where(hit, 1.0, 0.0))
        cur = jnp.where(hit, -jnp.inf, cur)
    es = [jnp.exp(v - vals[0]) for v in vals]
    denom = es[0]
    for e in es[1:]:
        denom = denom + e
    gates = [e / denom for e in es]
    ti = lax.broadcasted_iota(jnp.int32, (tm, tm), 0)
    si = lax.broadcasted_iota(jnp.int32, (tm, tm), 1)
    earlier = jnp.where(si < ti, 1.0, 0.0).astype(BF16)
    base = cnt_ref[...]
    ranks = []
    for oh in onehots:
        prefix = jnp.dot(earlier, oh.astype(BF16), preferred_element_type=F32)
        ranks.append(jnp.sum(oh * (base + prefix), axis=-1, keepdims=True))
        base = base + jnp.sum(oh, axis=0, keepdims=True)
    cnt_ref[...] = base
    out_lane = lax.broadcasted_iota(jnp.int32, route_ref.shape, 1)
    out = jnp.zeros(route_ref.shape, F32)
    for j, col in enumerate(ids + ranks + gates):
        out = jnp.where(out_lane == j, col, out)
    route_ref[...] = out


def _merge(x_all, mod, p_all, o_da, o_sc, y0, y1, bon0, bon1, p, n_out, n_lat, seq):
    n = n_out
    tm = 256
    n_lat_tiles = n_lat // tm
    tiles_per_seq = seq // tm
    n_batch = n_lat // seq

    def seg(i):
        return jnp.where(i < n_lat_tiles, i // tiles_per_seq, n_batch)

    def row(w):
        return pl.BlockSpec((tm, w), lambda i: (i, 0))

    def const(shape):
        nd = len(shape)
        return pl.BlockSpec(shape, lambda i: (0,) * nd)

    g2 = jnp.zeros((RW_GATE_PAD, RW_WIDTH), F32).at[:RW_GATE_LORA].set(p['rw_g2']).astype(BF16)
    head_of = jnp.arange(RW_WIDTH) // RW_HEAD_DIM
    gavg = (head_of[:, None] == head_of[None, :]).astype(F32) / RW_HEAD_DIM
    return pl.pallas_call(
        _merge_kernel,
        grid=(n // tm,),
        in_specs=[row(D_MODEL),
                  pl.BlockSpec((1, N_MOD, D_MODEL), lambda i: (seg(i), 0, 0)),
                  pl.BlockSpec((tm, 3 * D_MODEL), lambda i: (i, C_GL // (3 * D_MODEL))),
                  row(512), row(512), row(512), row(512), row(512), row(512),
                  pl.BlockSpec((tm, 512), lambda i: (i, C_LORA // 512)),
                  const((RW_GATE_PAD, RW_WIDTH)), const((1, RW_WIDTH)), const((1, RW_WIDTH)),
                  const((RW_WIDTH, RW_WIDTH)),
                  const((3, 512, D_MODEL)), const((D_MODEL, D_MODEL)), const((1, D_MODEL)),
                  const((D_MODEL, N_EXPERTS)), const((1, N_EXPERTS))],
        out_specs=[row(D_MODEL),
                   row(D_MODEL),
                   row(128),
                   const((1, N_EXPERTS))],
        out_shape=[jax.ShapeDtypeStruct((n, D_MODEL), F32),
                   jax.ShapeDtypeStruct((n, D_MODEL), F32),
                   jax.ShapeDtypeStruct((n, 128), F32),
                   jax.ShapeDtypeStruct((1, N_EXPERTS), F32)],
        compiler_params=_cparams(1),
        name="merge_router",
    )(x_all, mod, p_all, o_da, o_sc, y0, y1, bon0, bon1, p_all,
      g2, p['rw_lnx_g'].reshape(1, RW_WIDTH), p['rw_lnx_b'].reshape(1, RW_WIDTH), gavg.astype(BF16),
      p['w_branch'].astype(BF16), p['w_out'].astype(BF16), p['norm2_g'].reshape(1, D_MODEL),
      p['router_w'], p['router_b'].reshape(1, N_EXPERTS))


def _dispatch_kernel(pos_ref, h2_ref, xb_in, xb_out, sem, *, tm):
    del xb_in
    i = pl.program_id(0)

    def copy(t, row):
        return pltpu.make_async_copy(h2_ref.at[pl.ds(t, 1), :], xb_out.at[pl.ds(row, 1), :], sem.at[0])

    def issue(it, carry):
        for u in range(DMA_UNROLL // TOP_K):
            t = it * (DMA_UNROLL // TOP_K) + u
            base = (i * tm + t) * TOP_K
            for k in range(TOP_K):
                copy(t, pos_ref[base + k]).start(priority=k % 2)
        return carry

    def drain(it, carry):
        for _ in range(DMA_UNROLL):
            copy(0, 0).wait()
        return carry

    lax.fori_loop(0, tm * TOP_K // DMA_UNROLL, issue, 0)
    lax.fori_loop(0, tm * TOP_K // DMA_UNROLL, drain, 0)


def _dispatch(h2, pos, n_rows):
    n_tok = pos.shape[0] // TOP_K
    tm = 256
    grid_spec = pltpu.PrefetchScalarGridSpec(
        num_scalar_prefetch=1,
        grid=(n_tok // tm,),
        in_specs=[pl.BlockSpec((tm, D_MODEL), lambda i, pos: (i, 0)),
                  pl.BlockSpec(memory_space=pl.ANY)],
        out_specs=pl.BlockSpec(memory_space=pl.ANY),
        scratch_shapes=[pltpu.SemaphoreType.DMA((1,))],
    )
    return pl.pallas_call(
        functools.partial(_dispatch_kernel, tm=tm),
        grid_spec=grid_spec,
        out_shape=jax.ShapeDtypeStruct((n_rows, D_MODEL), F32),
        input_output_aliases={2: 0},
        compiler_params=pltpu.CompilerParams(dimension_semantics=("arbitrary",), vmem_limit_bytes=VMEM_LIMIT,
                                             disable_bounds_checks=True),
        name="moe_dispatch",
    )(pos, h2, jnp.zeros((n_rows, D_MODEL), F32))


def _expert_kernel(be_ref, nvalid_ref, x_ref, w1_ref, b1_ref, w2_ref, b2_ref, o_ref, w1b, w2b):
    i = pl.program_id(0)
    nvalid = nvalid_ref[0]

    @pl.when(i < nvalid)
    def _():
        @pl.when((i == 0) | (be_ref[i] != be_ref[jnp.maximum(i - 1, 0)]))
        def _():
            w1b[...] = w1_ref[0, 0].astype(BF16)
            w2b[...] = w2_ref[0, 0].astype(BF16)

        hid = jnp.dot(x_ref[...].astype(BF16), w1b[...], preferred_element_type=F32) + b1_ref[0, 0]
        glu = jnp.minimum(hid[:, :D_FF], SWIGLU_LIMIT)
        lin = jnp.clip(hid[:, D_FF:], -SWIGLU_LIMIT, SWIGLU_LIMIT)
        act = glu * _sigmoid(SWIGLU_ALPHA * glu) * (lin + 1.0)
        o_ref[...] = jnp.dot(act.astype(BF16), w2b[...], preferred_element_type=F32) + b2_ref[0, 0]

    @pl.when(i >= nvalid)
    def _():
        o_ref[...] = jnp.zeros_like(o_ref)


def _expert_ffn(xb, block_expert, nvalid, li, w1, b1, w2, b2):
    n_rows = xb.shape[0]
    mb = MOE_BLOCK
    depth = w1.shape[0]
    grid_spec = pltpu.PrefetchScalarGridSpec(
        num_scalar_prefetch=2,
        grid=(n_rows // mb,),
        in_specs=[pl.BlockSpec((mb, D_MODEL), lambda i, be, nv: (i, 0)),
                  pl.BlockSpec((1, 1, D_MODEL, 2 * D_FF), lambda i, be, nv: (li, be[i], 0, 0)),
                  pl.BlockSpec((1, 1, 1, 2 * D_FF), lambda i, be, nv: (li, be[i], 0, 0)),
                  pl.BlockSpec((1, 1, D_FF, D_MODEL), lambda i, be, nv: (li, be[i], 0, 0)),
                  pl.BlockSpec((1, 1, 1, D_MODEL), lambda i, be, nv: (li, be[i], 0, 0))],
        out_specs=pl.BlockSpec((mb, D_MODEL), lambda i, be, nv: (i, 0)),
        scratch_shapes=[pltpu.VMEM((D_MODEL, 2 * D_FF), BF16), pltpu.VMEM((D_FF, D_MODEL), BF16)],
    )
    return pl.pallas_call(
        _expert_kernel,
        grid_spec=grid_spec,
        out_shape=jax.ShapeDtypeStruct((n_rows, D_MODEL), F32),
        compiler_params=_cparams(1),
        name="expert_ffn",
    )(block_expert, nvalid, xb, w1, b1.reshape(depth, N_EXPERTS, 1, 2 * D_FF), w2,
      b2.reshape(depth, N_EXPERTS, 1, D_MODEL))


def _combine_kernel(pos_ref, yb_hbm, x_ref, route_ref, mod_ref, fg_ref, o_ref, buf, sem, *, tm, n_tiles, final_norm):
    i = pl.program_id(0)

    def copy(idx, slot, r):
        return pltpu.make_async_copy(yb_hbm.at[pl.ds(idx, 1), :], buf.at[slot, pl.ds(r, 1), :], sem.at[slot])

    def issue(tile, slot):
        def body(it, carry):
            for u in range(DMA_UNROLL):
                r = it * DMA_UNROLL + u
                copy(pos_ref[tile * (tm * TOP_K) + r], slot, r).start(priority=u % 2)
            return carry
        lax.fori_loop(0, tm * TOP_K // DMA_UNROLL, body, 0)

    @pl.when(i == 0)
    def _():
        issue(0, 0)

    @pl.when(i + 1 < n_tiles)
    def _():
        issue(i + 1, (i + 1) % 2)

    slot = i % 2

    def wait_body(it, carry):
        for _ in range(DMA_UNROLL):
            copy(0, slot, 0).wait()
        return carry
    lax.fori_loop(0, tm * TOP_K // DMA_UNROLL, wait_body, 0)

    f = route_ref[:, 2 * TOP_K:2 * TOP_K + 1] * buf[slot, pl.ds(0, tm), :]
    for kk in range(1, TOP_K):
        f = f + route_ref[:, 2 * TOP_K + kk:2 * TOP_K + kk + 1] * buf[slot, pl.ds(kk * tm, tm), :]
    x_new = x_ref[...] + mod_ref[0, 5:6, :] * f
    if final_norm:
        ms = jnp.mean(x_new * x_new, axis=-1, keepdims=True)
        x_new = x_new * lax.rsqrt(ms + NORM_EPS) * fg_ref[...]
    o_ref[...] = x_new


def _combine(yb, pos, x_all, route, mod, final_g, n_out, n_lat, seq, final_norm):
    tm = 256
    n_tiles = n_out // tm
    n_lat_tiles = n_lat // tm
    tiles_per_seq = seq // tm
    n_batch = n_lat // seq

    def seg(i):
        return jnp.where(i < n_lat_tiles, i // tiles_per_seq, n_batch)

    grid_spec = pltpu.PrefetchScalarGridSpec(
        num_scalar_prefetch=1,
        grid=(n_tiles,),
        in_specs=[pl.BlockSpec(memory_space=pl.ANY),
                  pl.BlockSpec((tm, D_MODEL), lambda i, pos: (i, 0)),
                  pl.BlockSpec((tm, 128), lambda i, pos: (i, 0)),
                  pl.BlockSpec((1, N_MOD, D_MODEL), lambda i, pos: (seg(i), 0, 0)),
                  pl.BlockSpec((1, D_MODEL), lambda i, pos: (0, 0))],
        out_specs=pl.BlockSpec((tm, D_MODEL), lambda i, pos: (i, 0)),
        scratch_shapes=[pltpu.VMEM((2, tm * TOP_K, D_MODEL), F32), pltpu.SemaphoreType.DMA((2,))],
    )
    return pl.pallas_call(
        functools.partial(_combine_kernel, tm=tm, n_tiles=n_tiles, final_norm=final_norm),
        grid_spec=grid_spec,
        out_shape=jax.ShapeDtypeStruct((n_out, D_MODEL), F32),
        compiler_params=pltpu.CompilerParams(dimension_semantics=("arbitrary",), vmem_limit_bytes=VMEM_LIMIT,
                                             disable_bounds_checks=True),
        name="moe_combine",
    )(pos, yb, x_all, route, mod, final_g.reshape(1, D_MODEL))


def _routing(route, counts):
    n_tok = route.shape[0]
    n_assign = n_tok * TOP_K
    counts = counts.reshape(N_EXPERTS).astype(jnp.int32)
    padded = (counts + MOE_BLOCK - 1) // MOE_BLOCK * MOE_BLOCK
    end_padded = jnp.cumsum(padded)
    start_padded = end_padded - padded
    n_rows = -(-n_assign // MOE_BLOCK) * MOE_BLOCK + N_EXPERTS * MOE_BLOCK
    n_blocks = n_rows // MOE_BLOCK
    block_start = jnp.arange(n_blocks, dtype=jnp.int32) * MOE_BLOCK
    block_expert = jnp.minimum(jnp.sum((end_padded[None, :] <= block_start[:, None]).astype(jnp.int32), axis=1),
                               N_EXPERTS - 1).astype(jnp.int32)
    nvalid = (end_padded[-1] // MOE_BLOCK).astype(jnp.int32).reshape(1)
    expert = route[:, :TOP_K].astype(jnp.int32)
    rank = route[:, TOP_K:2 * TOP_K].astype(jnp.int32)
    onehot = expert[:, :, None] == jnp.arange(N_EXPERTS, dtype=jnp.int32)[None, None, :]
    pos = rank + jnp.sum(jnp.where(onehot, start_padded[None, None, :], 0), axis=-1)
    return block_expert, nvalid, pos.reshape(-1).astype(jnp.int32), n_rows


def _layer(x_all, mod, li, p, tables, final_g, n_lat, seq, ctx_len, last):
    cos_t, sin_t = tables
    lam_init = 0.8 - 0.6 * math.exp(-0.3 * li)
    pad = jnp.zeros((D_MODEL, P_COLS - C_LORA - 4 * RW_LORA_W - RW_GATE_LORA), F32)
    w_in = p['w_in']
    w_all = jnp.concatenate(
        [w_in[:, 4608:], w_in[:, :4608], p['rw_w1'][0], p['rw_w1'][1], p['rw_a1'][0], p['rw_a1'][1],
         p['rw_g1'], pad], axis=1).astype(BF16)
    p_all = _projection(x_all, mod, p['norm1_g'], w_all, cos_t, sin_t, n_lat, seq)
    o_da = _attention(p_all, p['da_lambda'], p['da_subln_g'], lam_init, n_lat, seq, ctx_len, not last)
    n_tok = n_lat if last else x_all.shape[0]
    o_sc = _short_conv(p_all, p['conv_w'], n_tok, n_lat, seq, ctx_len)
    y0, bon0, y1, bon1 = _rwkv_scan(p_all, p, n_lat, seq, ctx_len)
    x_mid, h2, route, counts = _merge(x_all, mod, p_all, o_da, o_sc, y0, y1, bon0, bon1, p, n_tok, n_lat, seq)
    block_expert, nvalid, pos, n_rows = _routing(route, counts)
    xb = _dispatch(h2, pos, n_rows)
    yb = _expert_ffn(xb, block_expert, nvalid, li, p['exp_w1'], p['exp_b1'], p['exp_w2'], p['exp_b2'])
    tm = 256
    pos_by_slot = pos.reshape(n_tok // tm, tm, TOP_K).transpose(0, 2, 1).reshape(-1)
    return _combine(yb, pos_by_slot, x_mid, route, mod, final_g, n_tok, n_lat, seq, last)


def kernel(x, c, ctx, c_ctx, w_mod, b_mod, norm1_g, norm2_g, w_in, da_lambda, da_subln_g, conv_w, rw_w0, rw_w1, rw_w2, rw_a0, rw_a1, rw_a2, rw_g1, rw_g2, rw_k_k, rw_k_a, rw_r_k, rw_lnx_g, rw_lnx_b, w_branch, w_out, router_w, router_b, exp_w1, exp_b1, exp_w2, exp_b2, final_g):
    n_batch, seq, d = x.shape
    ctx_len = ctx.shape[1]
    depth = w_in.shape[0]
    n_lat = n_batch * seq
    assert d == D_MODEL and n_batch <= 8
    proj_tm = max(tm for tm in PROJ_TMS if seq % tm == 0 and (n_batch * ctx_len) % tm == 0)
    assert seq % ctx_len == 0 and ctx_len == 256
    x_all = jnp.concatenate([x.reshape(n_lat, d), ctx.reshape(n_batch * ctx_len, d)], axis=0)
    c_all = jnp.zeros((16, d), F32).at[:n_batch].set(c).at[n_batch].set(c_ctx)
    tables = _rope_tables(seq, proj_tm)
    for li in range(depth):
        p = {
            'norm1_g': norm1_g[li], 'norm2_g': norm2_g[li], 'w_in': w_in[li], 'da_lambda': da_lambda[li],
            'da_subln_g': da_subln_g[li], 'conv_w': conv_w[li],
            'rw_w0': rw_w0[li], 'rw_w1': rw_w1[li], 'rw_w2': rw_w2[li],
            'rw_a0': rw_a0[li], 'rw_a1': rw_a1[li], 'rw_a2': rw_a2[li],
            'rw_g1': rw_g1[li], 'rw_g2': rw_g2[li], 'rw_k_k': rw_k_k[li], 'rw_k_a': rw_k_a[li],
            'rw_r_k': rw_r_k[li], 'rw_lnx_g': rw_lnx_g[li], 'rw_lnx_b': rw_lnx_b[li],
            'w_branch': w_branch[li], 'w_out': w_out[li], 'router_w': router_w[li], 'router_b': router_b[li],
            'exp_w1': exp_w1, 'exp_b1': exp_b1, 'exp_w2': exp_w2, 'exp_b2': exp_b2,
        }
        mod = _modulation(c_all, w_mod[li], b_mod[li]).reshape(16, N_MOD, D_MODEL)
        x_all = _layer(x_all, mod, li, p, tables, final_g, n_lat, seq, ctx_len, li == depth - 1)
    return x_all.reshape(n_batch, seq, d)
```

```python
import functools
import math

import jax
import jax.numpy as jnp
from jax import lax
from jax.experimental import pallas as pl
from jax.experimental.pallas import tpu as pltpu

F32 = jnp.float32
BF16 = jnp.bfloat16

D_MODEL = 1024
GRID_W = 64
NORM_EPS = 1e-6
N_MOD = 6

DA_HEADS = 4
DA_HEAD_DIM = 64
DA_V_DIM = 128
DA_SUBLN_EPS = 1e-5
ROPE_BASE = 10000.0

SC_WIDTH = 512

RW_HEADS = 8
RW_HEAD_DIM = 64
RW_WIDTH = 512
RW_DECAY_SCALE = 0.606531
RW_GN_EPS = 64e-5
RW_CHUNK = 64
RW_LORA_W = 64
RW_GATE_LORA = 160
RW_GATE_PAD = 256

N_EXPERTS = 32
TOP_K = 4
D_FF = 1024
SWIGLU_LIMIT = 7.0
SWIGLU_ALPHA = 1.702
MOE_BLOCK = 512

C_GL = 0
C_Q = 3072
C_K = 3584
C_V = 4096
C_SB = 4608
C_SG = 5120
C_SX = 5632
C_RR = 6144
C_RK = 6656
C_RV = 7168
C_LORA = 7680
P_COLS = 8192

PROJ_TMS = (512, 1024)
PROJ_TN = 1024
ROPE_TILE = C_Q // PROJ_TN

VMEM_LIMIT = 56 * 1024 * 1024
DMA_UNROLL = 8


def _cparams(n_axes):
    return pltpu.CompilerParams(dimension_semantics=("arbitrary",) * n_axes, vmem_limit_bytes=VMEM_LIMIT)


def _sigmoid(x):
    return 1.0 / (1.0 + jnp.exp(-x))


def _nt_dot(a, b):
    return lax.dot_general(a, b, (((1,), (1,)), ((), ())), preferred_element_type=F32)


def _tn_dot(a, b):
    return lax.dot_general(a, b, (((0,), (0,)), ((), ())), preferred_element_type=F32)


def _mod_kernel(c_ref, w_ref, b_ref, o_ref):
    c = c_ref[...]
    s = c * _sigmoid(c)
    o_ref[...] = jnp.dot(s.astype(BF16), w_ref[...].astype(BF16), preferred_element_type=F32) + b_ref[...]


def _modulation(c_all, w_mod, b_mod):
    n = N_MOD * D_MODEL
    tn = 1536
    return pl.pallas_call(
        _mod_kernel,
        grid=(n // tn,),
        in_specs=[pl.BlockSpec((16, D_MODEL), lambda j: (0, 0)),
                  pl.BlockSpec((D_MODEL, tn), lambda j: (0, j)),
                  pl.BlockSpec((1, tn), lambda j: (0, j))],
        out_specs=pl.BlockSpec((16, tn), lambda j: (0, j)),
        out_shape=jax.ShapeDtypeStruct((16, n), F32),
        compiler_params=_cparams(1),
        name="modulation",
    )(c_all, w_mod, b_mod.reshape(1, n))


def _modnorm(x, g, shift, scale):
    ms = jnp.mean(x * x, axis=-1, keepdims=True)
    return (x * lax.rsqrt(ms + NORM_EPS) * g) * (1.0 + scale) + shift


def _proj_kernel(x_ref, mod_ref, g_ref, w_ref, cos_ref, sin_ref, o_ref, h_ref):
    j = pl.program_id(1)

    @pl.when(j == 0)
    def _():
        h = _modnorm(x_ref[...], g_ref[...], mod_ref[0, 0:1, :], mod_ref[0, 1:2, :])
        h_ref[...] = h.astype(BF16)

    @pl.when(j == ROPE_TILE)
    def _():
        acc = jnp.dot(h_ref[...], w_ref[...], preferred_element_type=F32)
        width = acc.shape[1]
        lane = lax.broadcasted_iota(jnp.int32, acc.shape, 1)
        fwd = pltpu.roll(acc, width - 16, axis=1)
        bwd = pltpu.roll(acc, 16, axis=1)
        rot = jnp.where((lane % 32) < 16, fwd, bwd)
        o_ref[...] = acc * cos_ref[...] + rot * sin_ref[...]

    @pl.when(j != ROPE_TILE)
    def _():
        o_ref[...] = jnp.dot(h_ref[...], w_ref[...], preferred_element_type=F32)


def _projection(x_all, mod, g, w_all, cos_t, sin_t, n_lat, seq):
    n = x_all.shape[0]
    tm, tn = cos_t.shape[0] - seq, PROJ_TN
    n_lat_tiles = n_lat // tm
    tiles_per_seq = seq // tm
    n_batch = n_lat // seq

    def seg(i):
        return jnp.where(i < n_lat_tiles, i // tiles_per_seq, n_batch)

    def rope_blk(i):
        return jnp.where(i < n_lat_tiles, i % tiles_per_seq, tiles_per_seq)

    return pl.pallas_call(
        _proj_kernel,
        grid=(n // tm, P_COLS // tn),
        in_specs=[pl.BlockSpec((tm, D_MODEL), lambda i, j: (i, 0)),
                  pl.BlockSpec((1, N_MOD, D_MODEL), lambda i, j: (seg(i), 0, 0)),
                  pl.BlockSpec((1, D_MODEL), lambda i, j: (0, 0)),
                  pl.BlockSpec((D_MODEL, tn), lambda i, j: (0, j)),
                  pl.BlockSpec((tm, tn), lambda i, j: (rope_blk(i), 0)),
                  pl.BlockSpec((tm, tn), lambda i, j: (rope_blk(i), 0))],
        out_specs=pl.BlockSpec((tm, tn), lambda i, j: (i, j)),
        out_shape=jax.ShapeDtypeStruct((n, P_COLS), F32),
        scratch_shapes=[pltpu.VMEM((tm, D_MODEL), BF16)],
        compiler_params=_cparams(2),
        name="norm_projection",
    )(x_all, mod, g.reshape(1, D_MODEL), w_all, cos_t, sin_t)


def _rope_tables(seq, tm):
    half = 16
    pos = jnp.arange(seq, dtype=jnp.int32)
    rows = (pos // GRID_W).astype(F32)
    cols = (pos % GRID_W).astype(F32)
    inv = jnp.power(ROPE_BASE, -jnp.arange(half, dtype=F32) / half)
    ar = rows[:, None] * inv[None, :]
    ac = cols[:, None] * inv[None, :]
    cos64 = jnp.concatenate([jnp.cos(ar), jnp.cos(ar), jnp.cos(ac), jnp.cos(ac)], axis=-1)
    sin64 = jnp.concatenate([-jnp.sin(ar), jnp.sin(ar), -jnp.sin(ac), jnp.sin(ac)], axis=-1)
    reps = 512 // 64
    cos512 = jnp.tile(cos64, (1, reps))
    sin512 = jnp.tile(sin64, (1, reps))
    qs = DA_HEAD_DIM ** -0.5
    cos_l = jnp.concatenate([cos512 * qs, cos512], axis=-1)
    sin_l = jnp.concatenate([sin512 * qs, sin512], axis=-1)
    cos_c = jnp.concatenate([jnp.full((tm, 512), qs, F32), jnp.ones((tm, 512), F32)], axis=-1)
    sin_c = jnp.zeros((tm, 1024), F32)
    return jnp.concatenate([cos_l, cos_c], axis=0), jnp.concatenate([sin_l, sin_c], axis=0)


def _attn_kernel(lam_ref, g_ref, q_ref, kl_ref, vl_ref, kc_ref, vc_ref, o_ref, kb_ref, vb_ref, *,
                 lam_init, n_lat_tiles):
    i = pl.program_id(2)
    seq = kl_ref.shape[0]

    @pl.when(i == 0)
    def _():
        kb_ref[0:seq, :] = kl_ref[...].astype(BF16)
        kb_ref[seq:, :] = kc_ref[...].astype(BF16)
        vb_ref[0:seq, :] = vl_ref[...].astype(BF16)
        vb_ref[seq:, :] = vc_ref[...].astype(BF16)

    @pl.when(i < n_lat_tiles)
    def _():
        _attend(lam_ref, g_ref, q_ref, kb_ref[...], vb_ref[...], o_ref, lam_init)

    @pl.when(i >= n_lat_tiles)
    def _():
        _attend(lam_ref, g_ref, q_ref, kb_ref[seq:, :], vb_ref[seq:, :], o_ref, lam_init)


def _attend(lam_ref, g_ref, q_ref, k, v, o_ref, lam_init):
    q = q_ref[...]
    tq = q.shape[0]
    lane = lax.broadcasted_iota(jnp.int32, q.shape, 1)
    q12 = jnp.concatenate([jnp.where(lane < DA_HEAD_DIM, q, 0.0), jnp.where(lane >= DA_HEAD_DIM, q, 0.0)],
                          axis=0).astype(BF16)
    s = _nt_dot(q12, k)
    e = jnp.exp(s - jnp.max(s, axis=-1, keepdims=True))
    o12 = jnp.dot(e.astype(BF16), v, preferred_element_type=F32) / jnp.sum(e, axis=-1, keepdims=True)
    lv = lam_ref[...]
    lam = (jnp.exp(jnp.sum(lv[0:1] * lv[1:2], axis=-1, keepdims=True))
           - jnp.exp(jnp.sum(lv[2:3] * lv[3:4], axis=-1, keepdims=True)) + lam_init)
    o = o12[:tq] - lam * o12[tq:]
    ms = jnp.mean(o * o, axis=-1, keepdims=True)
    o_ref[...] = (o * lax.rsqrt(ms + DA_SUBLN_EPS) * g_ref[...]) * (1.0 - lam_init)


def _attention(p_all, da_lambda, subln_g, lam_init, n_lat, seq, ctx_len, with_ctx_queries):
    n = p_all.shape[0]
    n_batch = n_lat // seq
    hw = DA_V_DIM
    cq, ck, cv = C_Q // hw, C_K // hw, C_V // hw
    tq = ctx_len
    ctx_blk0 = n_lat // ctx_len
    qpb = seq // tq
    n_out = n if with_ctx_queries else n_lat

    def q_blk(b, i):
        return jnp.where(i < qpb, b * qpb + i, ctx_blk0 + b)

    return pl.pallas_call(
        functools.partial(_attn_kernel, lam_init=lam_init, n_lat_tiles=qpb),
        grid=(n_batch, DA_HEADS, qpb + (1 if with_ctx_queries else 0)),
        in_specs=[pl.BlockSpec((4, DA_HEAD_DIM), lambda b, h, i: (0, 0)),
                  pl.BlockSpec((1, hw), lambda b, h, i: (0, 0)),
                  pl.BlockSpec((tq, hw), lambda b, h, i: (q_blk(b, i), cq + h)),
                  pl.BlockSpec((seq, hw), lambda b, h, i: (b, ck + h)),
                  pl.BlockSpec((seq, hw), lambda b, h, i: (b, cv + h)),
                  pl.BlockSpec((ctx_len, hw), lambda b, h, i: (ctx_blk0 + b, ck + h)),
                  pl.BlockSpec((ctx_len, hw), lambda b, h, i: (ctx_blk0 + b, cv + h))],
        out_specs=pl.BlockSpec((tq, hw), lambda b, h, i: (q_blk(b, i), h)),
        out_shape=jax.ShapeDtypeStruct((n_out, DA_HEADS * hw), F32),
        scratch_shapes=[pltpu.VMEM((seq + ctx_len, hw), BF16), pltpu.VMEM((seq + ctx_len, hw), BF16)],
        compiler_params=_cparams(3),
        name="diff_attention",
    )(da_lambda, subln_g.reshape(1, hw), p_all, p_all, p_all, p_all, p_all)


def _conv_kernel(sb_ref, sg_ref, sx_ref, gp_ref, xp_ref, gn_ref, xn_ref, w_ref, o_ref, *, n_lat_tiles, tiles_per_seq):
    i = pl.program_id(0)
    starts = (i >= n_lat_tiles) | (i % tiles_per_seq == 0)
    ends = (i >= n_lat_tiles) | (i % tiles_per_seq == tiles_per_seq - 1)
    u = sg_ref[...] * sx_ref[...]
    length = u.shape[0]
    halo_prev = jnp.where(starts, 0.0, gp_ref[7:8, :] * xp_ref[7:8, :])
    halo_next = jnp.where(ends, 0.0, gn_ref[0:1, :] * xn_ref[0:1, :])
    t = lax.broadcasted_iota(jnp.int32, u.shape, 0)
    prev = jnp.where(t == 0, halo_prev, pltpu.roll(u, 1, axis=0))
    nxt = jnp.where(t == length - 1, halo_next, pltpu.roll(u, length - 1, axis=0))
    w = w_ref[...]
    o_ref[...] = sb_ref[...] * (w[0:1] * prev + w[1:2] * u + w[2:3] * nxt)


def _short_conv(p_all, conv_w, n_out, n_lat, seq, ctx_len):
    n = p_all.shape[0]
    tm = ctx_len
    tc = 256
    ncol = SC_WIDTH // tc
    cb, cg, cx = C_SB // tc, C_SG // tc, C_SX // tc
    sub = tm // 8
    last8 = n // 8 - 1

    def main(col):
        return pl.BlockSpec((tm, tc), lambda i, j: (i, col + j))

    def halo_prev(col):
        return pl.BlockSpec((8, tc), lambda i, j: (jnp.maximum(i * sub - 1, 0), col + j))

    def halo_next(col):
        return pl.BlockSpec((8, tc), lambda i, j: (jnp.minimum((i + 1) * sub, last8), col + j))

    return pl.pallas_call(
        functools.partial(_conv_kernel, n_lat_tiles=n_lat // tm, tiles_per_seq=seq // tm),
        grid=(n_out // tm, ncol),
        in_specs=[main(cb), main(cg), main(cx), halo_prev(cg), halo_prev(cx), halo_next(cg), halo_next(cx),
                  pl.BlockSpec((3, tc), lambda i, j: (0, j))],
        out_specs=pl.BlockSpec((tm, tc), lambda i, j: (i, j)),
        out_shape=jax.ShapeDtypeStruct((n_out, SC_WIDTH), F32),
        compiler_params=_cparams(2),
        name="short_conv",
    )(p_all, p_all, p_all, p_all, p_all, p_all, p_all, conv_w)


def _bdot(a, b):
    return jnp.dot(a.astype(BF16), b.astype(BF16), preferred_element_type=F32)


def _block_unit_inverses(lmats, xor_idx):
    eye = jnp.where(xor_idx == 0, 1.0, 0.0)
    lds = [jnp.where(xor_idx < 8, l, 0.0).astype(BF16) for l in lmats]
    ld2s = [jnp.dot(ld, ld, preferred_element_type=F32).astype(BF16) for ld in lds]
    xs = [eye + ld.astype(F32) for ld in lds]
    xs = [x + jnp.dot(x.astype(BF16), ld2, preferred_element_type=F32) for x, ld2 in zip(xs, ld2s)]
    ld4s = [jnp.dot(ld2, ld2, preferred_element_type=F32).astype(BF16) for ld2 in ld2s]
    xs = [x + jnp.dot(x.astype(BF16), ld4, preferred_element_type=F32) for x, ld4 in zip(xs, ld4s)]
    b = 8
    while b < RW_CHUNK:
        offs = [jnp.where(xor_idx >= b, jnp.where(xor_idx < 2 * b, l, 0.0), 0.0).astype(BF16) for l in lmats]
        xbs = [x.astype(BF16) for x in xs]
        ts = [jnp.dot(xb, off, preferred_element_type=F32).astype(BF16) for xb, off in zip(xbs, offs)]
        xs = [x + jnp.dot(t, xb, preferred_element_type=F32) for x, t, xb in zip(xs, ts, xbs)]
        b *= 2
    return xs


def _rwkv_prepare(r, k, v, lo, w0, w2, a0, a2, k_k, k_a, r_k, gsum, direction):
    n = RW_CHUNK
    reverse = direction == 1
    dw = lo[:, direction * RW_LORA_W:(direction + 1) * RW_LORA_W]
    da = lo[:, (2 + direction) * RW_LORA_W:(3 + direction) * RW_LORA_W]
    wl = w0 + jnp.dot(jnp.tanh(dw).astype(BF16), w2.astype(BF16), preferred_element_type=F32)
    logw = -RW_DECAY_SCALE * _sigmoid(wl)
    alpha = _sigmoid(a0 + jnp.dot(da.astype(BF16), a2.astype(BF16), preferred_element_type=F32))
    keff = k * (1.0 + (alpha - 1.0) * k_a)
    kkraw = k * k_k
    kk = kkraw * lax.rsqrt(jnp.maximum(_split_dot(kkraw * kkraw, gsum), 1e-24))
    bonus = _split_dot(r * keff * r_k, gsum) * v
    ti = lax.broadcasted_iota(jnp.int32, (n, n), 0)
    si = lax.broadcasted_iota(jnp.int32, (n, n), 1)
    upto_n = (si >= ti) if reverse else (si <= ti)
    cl = jnp.dot(upto_n.astype(F32), logw, preferred_element_type=F32, precision=lax.Precision.HIGHEST)
    p_in = jnp.exp(cl)
    p_inv = jnp.exp(-cl)
    last = 0 if reverse else n - 1
    p_end = p_in[last:last + 1, :]
    b_t = kk * alpha * p_inv
    k_t = keff * p_inv
    return dict(v=v, a=-kk * jnp.exp(cl - logw), b=b_t, k=k_t, r=r * p_in, b_end=b_t * p_end, k_end=k_t * p_end,
                p_end=p_end, bonus=bonus)


def _rwkv_kernel(tok0_ref, tok1_ref, w0_ref, w2_ref, a0_ref, a2_ref, kk_ref, ka_ref, rkp_ref, gsum_ref,
                 out0_ref, out1_ref, s_ref):
    c = pl.program_id(1)

    @pl.when(c == 0)
    def _():
        s_ref[...] = jnp.zeros_like(s_ref)

    n = RW_CHUNK
    hd = RW_HEAD_DIM
    cw = RW_WIDTH
    tok_refs = (tok0_ref, tok1_ref)
    out_refs = (out0_ref, out1_ref)
    gsum = gsum_ref[...]
    prep = []
    for d in range(2):
        rr, rk, rv, lo = [tok_refs[d][:, j * cw:(j + 1) * cw] for j in range(4)]
        prep.append(_rwkv_prepare(rr, rk, rv, lo, w0_ref[d], w2_ref[d], a0_ref[d], a2_ref[d],
                                  kk_ref[...], ka_ref[...], rkp_ref[...], gsum, d))
        out_refs[d][:, cw:] = prep[d]['bonus']

    m2 = 2 * n
    t2 = lax.broadcasted_iota(jnp.int32, (m2, m2), 0)
    s2 = lax.broadcasted_iota(jnp.int32, (m2, m2), 1)
    xor_idx = t2 ^ s2
    same_head = xor_idx < n
    before = (same_head & (s2 < t2), same_head & (s2 > t2))
    upto = (same_head & (s2 <= t2), same_head & (s2 >= t2))
    first_head = lax.broadcasted_iota(jnp.int32, (n, m2), 1) < hd

    def stack(x):
        return jnp.concatenate([jnp.where(first_head, x, 0.0), jnp.where(first_head, 0.0, x)], axis=0)

    n_pairs = RW_HEADS // 2
    units = [(d, pr) for d in range(2) for pr in range(n_pairs)]

    def part(name, d, pr):
        return prep[d][name][:, pr * m2:(pr + 1) * m2]

    vss = [stack(part('v', d, pr)) for d, pr in units]
    lhss = [jnp.concatenate([stack(part('a', d, pr)), stack(part('r', d, pr))], axis=0).astype(BF16)
            for d, pr in units]
    rhss = [jnp.concatenate([stack(part('b', d, pr)), stack(part('k', d, pr))], axis=0).astype(BF16)
            for d, pr in units]
    grams = [_nt_dot(lhs, rhs) for lhs, rhs in zip(lhss, rhss)]
    s0s = [s_ref[d, pr] for d, pr in units]
    on_states = [_nt_dot(lhs, s0.astype(BF16)) for lhs, s0 in zip(lhss, s0s)]
    l_abs = [jnp.where(before[d], g[:m2, :m2], 0.0) for (d, _), g in zip(units, grams)]
    l_aks = [jnp.where(before[d], g[:m2, m2:], 0.0).astype(BF16) for (d, _), g in zip(units, grams)]
    m_alls = [jnp.concatenate([jnp.where(upto[d], g[m2:, :m2], 0.0), jnp.where(upto[d], g[m2:, m2:], 0.0)],
                              axis=1).astype(BF16) for (d, _), g in zip(units, grams)]
    u_rhss = [os_[:m2] + jnp.dot(l_ak, vs.astype(BF16), preferred_element_type=F32)
              for os_, l_ak, vs in zip(on_states, l_aks, vss)]
    tinvs = _block_unit_inverses(l_abs, xor_idx)
    us = [_bdot(tinv, u_rhs) for tinv, u_rhs in zip(tinvs, u_rhss)]
    uvs = [jnp.concatenate([u, vs], axis=0).astype(BF16) for u, vs in zip(us, vss)]
    y2s = [os_[m2:] + jnp.dot(m_all, uv, preferred_element_type=F32) for os_, m_all, uv in zip(on_states, m_alls, uvs)]
    bks = [jnp.concatenate([stack(part('b_end', d, pr)), stack(part('k_end', d, pr))], axis=0).astype(BF16)
           for d, pr in units]
    for i, (d, pr) in enumerate(units):
        s_ref[d, pr] = s0s[i] * part('p_end', d, pr) + _tn_dot(uvs[i], bks[i])
    for d in range(2):
        out_refs[d][:, :cw] = jnp.concatenate(
            [y2s[d * n_pairs + pr][:n] + y2s[d * n_pairs + pr][n:] for pr in range(n_pairs)], axis=-1)


def _rwkv_scan(p_all, p, n_lat, seq, ctx_len):
    n = p_all.shape[0]
    n_batch = n_lat // seq
    ch = RW_CHUNK
    ctx_c = ctx_len // ch
    seq_c = seq // ch
    lat0 = n_lat // ch
    cw = RW_WIDTH

    def blk_fwd(b, c):
        return jnp.where(c < ctx_c, lat0 + b * ctx_c + c, b * seq_c + (c - ctx_c))

    def blk_bwd(b, c):
        return jnp.where(c < ctx_c, lat0 + b * ctx_c + (ctx_c - 1 - c), b * seq_c + (seq_c - 1 - (c - ctx_c)))

    tok_w = 4 * cw
    assert (C_RK, C_RV, C_LORA) == (C_RR + cw, C_RR + 2 * cw, C_RR + 3 * cw) and C_RR % tok_w == 0

    def tok_spec(blk):
        return pl.BlockSpec((ch, tok_w), lambda b, c: (blk(b, c), C_RR // tok_w))

    def par_spec(shape):
        nd = len(shape)
        return pl.BlockSpec(shape, lambda b, c: (0,) * nd)

    def out_spec(blk):
        return pl.BlockSpec((ch, 2 * cw), lambda b, c: (blk(b, c), 0))

    sds = jax.ShapeDtypeStruct((n, 2 * cw), F32)
    return pl.pallas_call(
        _rwkv_kernel,
        grid=(n_batch, ctx_c + seq_c),
        in_specs=[tok_spec(blk_fwd), tok_spec(blk_bwd),
                  par_spec((2, 1, cw)), par_spec((2, RW_LORA_W, cw)), par_spec((2, 1, cw)), par_spec((2, RW_LORA_W, cw)),
                  par_spec((1, cw)), par_spec((1, cw)), par_spec((1, cw)), par_spec((cw, cw))],
        out_specs=[out_spec(blk_fwd), out_spec(blk_bwd)],
        out_shape=[sds, sds],
        scratch_shapes=[pltpu.VMEM((2, RW_HEADS // 2, 2 * RW_HEAD_DIM, 2 * RW_HEAD_DIM), F32)],
        compiler_params=_cparams(2),
        name="rwkv_scan",
    )(p_all, p_all,
      p['rw_w0'].reshape(2, 1, cw), p['rw_w2'], p['rw_a0'].reshape(2, 1, cw), p['rw_a2'],
      p['rw_k_k'].reshape(1, cw), p['rw_k_a'].reshape(1, cw), p['rw_r_k'].reshape(1, cw), _head_group_ones())


def _head_group_ones():
    head_of = jnp.arange(RW_WIDTH) // RW_HEAD_DIM
    return (head_of[:, None] == head_of[None, :]).astype(BF16)


def _split_dot(x, w_bf16):
    hi = x.astype(BF16)
    lo = (x - hi.astype(F32)).astype(BF16)
    return (jnp.dot(hi, w_bf16, preferred_element_type=F32) + jnp.dot(lo, w_bf16, preferred_element_type=F32))


def _merge_kernel(x_ref, mod_ref, gl_ref, oda_ref, osc_ref, rw0_ref, rw1_ref, lo_ref,
                  g2_ref, lng_ref, lnb_ref, gavg_ref, wb_ref, wo_ref, n2g_ref, rw_ref, rb_ref,
                  xo_ref, h2_ref, route_ref, cnt_ref):
    @pl.when(pl.program_id(0) == 0)
    def _():
        cnt_ref[...] = jnp.zeros_like(cnt_ref)

    ysum = rw0_ref[:, :RW_WIDTH] + rw1_ref[:, :RW_WIDTH]
    gavg = gavg_ref[...]
    mu = _split_dot(ysum, gavg)
    yc = ysum - mu
    var = _split_dot(yc * yc, gavg)
    yn = yc * lax.rsqrt(var + RW_GN_EPS) * lng_ref[...] + lnb_ref[...]
    dg = lo_ref[:, 4 * RW_LORA_W:4 * RW_LORA_W + RW_GATE_PAD]
    gate = jnp.dot(_sigmoid(dg).astype(BF16), g2_ref[...], preferred_element_type=F32)
    o_rw = (yn + rw0_ref[:, RW_WIDTH:] + rw1_ref[:, RW_WIDTH:]) * gate

    branches = (oda_ref[...], osc_ref[...], o_rw)
    merged = None
    for i, br in enumerate(branches):
        proj = jnp.dot(br.astype(BF16), wb_ref[i], preferred_element_type=F32)
        term = _sigmoid(gl_ref[:, i * D_MODEL:(i + 1) * D_MODEL]) * proj
        merged = term if merged is None else merged + term
    y = jnp.dot(merged.astype(BF16), wo_ref[...], preferred_element_type=F32)
    x_new = x_ref[...] + mod_ref[0, 2:3, :] * y
    xo_ref[...] = x_new
    h2 = _modnorm(x_new, n2g_ref[...], mod_ref[0, 3:4, :], mod_ref[0, 4:5, :])
    h2_ref[...] = h2
    logits = (jnp.dot(h2, rw_ref[...], preferred_element_type=F32, precision=lax.Precision.HIGHEST)
              + rb_ref[...])
    _route(logits, route_ref, cnt_ref)


def _route(logits, route_ref, cnt_ref):
    tm, ne = logits.shape
    lane = lax.broadcasted_iota(jnp.int32, logits.shape, 1)
    cur = logits
    vals, onehots, ids = [], [], []
    for _ in range(TOP_K):
        m = jnp.max(cur, axis=-1, keepdims=True)
        idx = jnp.min(jnp.where(cur == m, lane, ne), axis=-1, keepdims=True)
        hit = lane == idx
        vals.append(m)
        ids.append(idx.astype(F32))
        onehots.append(jnp.where(hit, 1.0, 0.0))
        cur = jnp.where(hit, -jnp.inf, cur)
    es = [jnp.exp(v - vals[0]) for v in vals]
    denom = es[0]
    for e in es[1:]:
        denom = denom + e
    gates = [e / denom for e in es]
    ti = lax.broadcasted_iota(jnp.int32, (tm, tm), 0)
    si = lax.broadcasted_iota(jnp.int32, (tm, tm), 1)
    earlier = jnp.where(si < ti, 1.0, 0.0).astype(BF16)
    base = cnt_ref[...]
    ranks = []
    for oh in onehots:
        prefix = jnp.dot(earlier, oh.astype(BF16), preferred_element_type=F32)
        ranks.append(jnp.sum(oh * (base + prefix), axis=-1, keepdims=True))
        base = base + jnp.sum(oh, axis=0, keepdims=True)
    cnt_ref[...] = base
    out_lane = lax.broadcasted_iota(jnp.int32, route_ref.shape, 1)
    out = jnp.zeros(route_ref.shape, F32)
    for j, col in enumerate(ids + ranks + gates):
        out = jnp.where(out_lane == j, col, out)
    route_ref[...] = out


def _merge(x_all, mod, p_all, o_da, o_sc, rw_fwd, rw_bwd, p, n_out, n_lat, seq):
    n = n_out
    tm = 256
    n_lat_tiles = n_lat // tm
    tiles_per_seq = seq // tm
    n_batch = n_lat // seq

    def seg(i):
        return jnp.where(i < n_lat_tiles, i // tiles_per_seq, n_batch)

    def row(w):
        return pl.BlockSpec((tm, w), lambda i: (i, 0))

    def const(shape):
        nd = len(shape)
        return pl.BlockSpec(shape, lambda i: (0,) * nd)

    g2 = jnp.zeros((RW_GATE_PAD, RW_WIDTH), F32).at[:RW_GATE_LORA].set(p['rw_g2']).astype(BF16)
    head_of = jnp.arange(RW_WIDTH) // RW_HEAD_DIM
    gavg = (head_of[:, None] == head_of[None, :]).astype(F32) / RW_HEAD_DIM
    return pl.pallas_call(
        _merge_kernel,
        grid=(n // tm,),
        in_specs=[row(D_MODEL),
                  pl.BlockSpec((1, N_MOD, D_MODEL), lambda i: (seg(i), 0, 0)),
                  pl.BlockSpec((tm, 3 * D_MODEL), lambda i: (i, C_GL // (3 * D_MODEL))),
                  row(512), row(512), row(2 * RW_WIDTH), row(2 * RW_WIDTH),
                  pl.BlockSpec((tm, 512), lambda i: (i, C_LORA // 512)),
                  const((RW_GATE_PAD, RW_WIDTH)), const((1, RW_WIDTH)), const((1, RW_WIDTH)),
                  const((RW_WIDTH, RW_WIDTH)),
                  const((3, 512, D_MODEL)), const((D_MODEL, D_MODEL)), const((1, D_MODEL)),
                  const((D_MODEL, N_EXPERTS)), const((1, N_EXPERTS))],
        out_specs=[row(D_MODEL),
                   row(D_MODEL),
                   row(128),
                   const((1, N_EXPERTS))],
        out_shape=[jax.ShapeDtypeStruct((n, D_MODEL), F32),
                   jax.ShapeDtypeStruct((n, D_MODEL), F32),
                   jax.ShapeDtypeStruct((n, 128), F32),
                   jax.ShapeDtypeStruct((1, N_EXPERTS), F32)],
        compiler_params=_cparams(1),
        name="merge_router",
    )(x_all, mod, p_all, o_da, o_sc, rw_fwd, rw_bwd, p_all,
      g2, p['rw_lnx_g'].reshape(1, RW_WIDTH), p['rw_lnx_b'].reshape(1, RW_WIDTH), gavg.astype(BF16),
      p['w_branch'].astype(BF16), p['w_out'].astype(BF16), p['norm2_g'].reshape(1, D_MODEL),
      p['router_w'], p['router_b'].reshape(1, N_EXPERTS))


def _dispatch_kernel(pos_ref, h2_ref, xb_in, xb_out, sem, *, tm):
    del xb_in
    i = pl.program_id(0)

    def copy(t, row):
        return pltpu.make_async_copy(h2_ref.at[pl.ds(t, 1), :], xb_out.at[pl.ds(row, 1), :], sem.at[0])

    def issue(it, carry):
        for u in range(DMA_UNROLL // TOP_K):
            t = it * (DMA_UNROLL // TOP_K) + u
            base = (i * tm + t) * TOP_K
            for k in range(TOP_K):
                copy(t, pos_ref[base + k]).start(priority=k % 2)
        return carry

    def drain(it, carry):
        for _ in range(DMA_UNROLL):
            copy(0, 0).wait()
        return carry

    lax.fori_loop(0, tm * TOP_K // DMA_UNROLL, issue, 0)
    lax.fori_loop(0, tm * TOP_K // DMA_UNROLL, drain, 0)


def _dispatch(h2, pos, n_rows):
    n_tok = pos.shape[0] // TOP_K
    tm = 256
    grid_spec = pltpu.PrefetchScalarGridSpec(
        num_scalar_prefetch=1,
        grid=(n_tok // tm,),
        in_specs=[pl.BlockSpec((tm, D_MODEL), lambda i, pos: (i, 0)),
                  pl.BlockSpec(memory_space=pl.ANY)],
        out_specs=pl.BlockSpec(memory_space=pl.ANY),
        scratch_shapes=[pltpu.SemaphoreType.DMA((1,))],
    )
    return pl.pallas_call(
        functools.partial(_dispatch_kernel, tm=tm),
        grid_spec=grid_spec,
        out_shape=jax.ShapeDtypeStruct((n_rows, D_MODEL), F32),
        input_output_aliases={2: 0},
        compiler_params=pltpu.CompilerParams(dimension_semantics=("arbitrary",), vmem_limit_bytes=VMEM_LIMIT,
                                             disable_bounds_checks=True),
        name="moe_dispatch",
    )(pos, h2, jnp.zeros((n_rows, D_MODEL), F32))


def _expert_kernel(be_ref, nvalid_ref, x_ref, w1_ref, b1_ref, w2_ref, b2_ref, o_ref, w1b, w2b):
    i = pl.program_id(0)
    nvalid = nvalid_ref[0]

    @pl.when(i < nvalid)
    def _():
        @pl.when((i == 0) | (be_ref[i] != be_ref[jnp.maximum(i - 1, 0)]))
        def _():
            w1b[...] = w1_ref[0, 0].astype(BF16)
            w2b[...] = w2_ref[0, 0].astype(BF16)

        hid = jnp.dot(x_ref[...].astype(BF16), w1b[...], preferred_element_type=F32) + b1_ref[0, 0]
        glu = jnp.minimum(hid[:, :D_FF], SWIGLU_LIMIT)
        lin = jnp.clip(hid[:, D_FF:], -SWIGLU_LIMIT, SWIGLU_LIMIT)
        act = glu * _sigmoid(SWIGLU_ALPHA * glu) * (lin + 1.0)
        o_ref[...] = jnp.dot(act.astype(BF16), w2b[...], preferred_element_type=F32) + b2_ref[0, 0]

    @pl.when(i >= nvalid)
    def _():
        o_ref[...] = jnp.zeros_like(o_ref)


def _expert_ffn(xb, block_expert, nvalid, li, w1, b1, w2, b2):
    n_rows = xb.shape[0]
    mb = MOE_BLOCK
    depth = w1.shape[0]
    grid_spec = pltpu.PrefetchScalarGridSpec(
        num_scalar_prefetch=2,
        grid=(n_rows // mb,),
        in_specs=[pl.BlockSpec((mb, D_MODEL), lambda i, be, nv: (i, 0)),
                  pl.BlockSpec((1, 1, D_MODEL, 2 * D_FF), lambda i, be, nv: (li, be[i], 0, 0)),
                  pl.BlockSpec((1, 1, 1, 2 * D_FF), lambda i, be, nv: (li, be[i], 0, 0)),
                  pl.BlockSpec((1, 1, D_FF, D_MODEL), lambda i, be, nv: (li, be[i], 0, 0)),
                  pl.BlockSpec((1, 1, 1, D_MODEL), lambda i, be, nv: (li, be[i], 0, 0))],
        out_specs=pl.BlockSpec((mb, D_MODEL), lambda i, be, nv: (i, 0)),
        scratch_shapes=[pltpu.VMEM((D_MODEL, 2 * D_FF), BF16), pltpu.VMEM((D_FF, D_MODEL), BF16)],
    )
    return pl.pallas_call(
        _expert_kernel,
        grid_spec=grid_spec,
        out_shape=jax.ShapeDtypeStruct((n_rows, D_MODEL), F32),
        compiler_params=_cparams(1),
        name="expert_ffn",
    )(block_expert, nvalid, xb, w1, b1.reshape(depth, N_EXPERTS, 1, 2 * D_FF), w2,
      b2.reshape(depth, N_EXPERTS, 1, D_MODEL))


def _combine_kernel(pos_ref, yb_hbm, x_ref, route_ref, mod_ref, fg_ref, o_ref, buf, sem, *, tm, n_tiles, final_norm):
    i = pl.program_id(0)

    def copy(idx, slot, r):
        return pltpu.make_async_copy(yb_hbm.at[pl.ds(idx, 1), :], buf.at[slot, pl.ds(r, 1), :], sem.at[slot])

    def issue(tile, slot):
        def body(it, carry):
            for u in range(DMA_UNROLL):
                r = it * DMA_UNROLL + u
                copy(pos_ref[tile * (tm * TOP_K) + r], slot, r).start(priority=u % 2)
            return carry
        lax.fori_loop(0, tm * TOP_K // DMA_UNROLL, body, 0)

    @pl.when(i == 0)
    def _():
        issue(0, 0)

    @pl.when(i + 1 < n_tiles)
    def _():
        issue(i + 1, (i + 1) % 2)

    slot = i % 2

    def wait_body(it, carry):
        for _ in range(DMA_UNROLL):
            copy(0, slot, 0).wait()
        return carry
    lax.fori_loop(0, tm * TOP_K // DMA_UNROLL, wait_body, 0)

    f = route_ref[:, 2 * TOP_K:2 * TOP_K + 1] * buf[slot, pl.ds(0, tm), :]
    for kk in range(1, TOP_K):
        f = f + route_ref[:, 2 * TOP_K + kk:2 * TOP_K + kk + 1] * buf[slot, pl.ds(kk * tm, tm), :]
    x_new = x_ref[...] + mod_ref[0, 5:6, :] * f
    if final_norm:
        ms = jnp.mean(x_new * x_new, axis=-1, keepdims=True)
        x_new = x_new * lax.rsqrt(ms + NORM_EPS) * fg_ref[...]
    o_ref[...] = x_new


def _combine(yb, pos, x_all, route, mod, final_g, n_out, n_lat, seq, final_norm):
    tm = 256
    n_tiles = n_out // tm
    n_lat_tiles = n_lat // tm
    tiles_per_seq = seq // tm
    n_batch = n_lat // seq

    def seg(i):
        return jnp.where(i < n_lat_tiles, i // tiles_per_seq, n_batch)

    grid_spec = pltpu.PrefetchScalarGridSpec(
        num_scalar_prefetch=1,
        grid=(n_tiles,),
        in_specs=[pl.BlockSpec(memory_space=pl.ANY),
                  pl.BlockSpec((tm, D_MODEL), lambda i, pos: (i, 0)),
                  pl.BlockSpec((tm, 128), lambda i, pos: (i, 0)),
                  pl.BlockSpec((1, N_MOD, D_MODEL), lambda i, pos: (seg(i), 0, 0)),
                  pl.BlockSpec((1, D_MODEL), lambda i, pos: (0, 0))],
        out_specs=pl.BlockSpec((tm, D_MODEL), lambda i, pos: (i, 0)),
        scratch_shapes=[pltpu.VMEM((2, tm * TOP_K, D_MODEL), F32), pltpu.SemaphoreType.DMA((2,))],
    )
    return pl.pallas_call(
        functools.partial(_combine_kernel, tm=tm, n_tiles=n_tiles, final_norm=final_norm),
        grid_spec=grid_spec,
        out_shape=jax.ShapeDtypeStruct((n_out, D_MODEL), F32),
        compiler_params=pltpu.CompilerParams(dimension_semantics=("arbitrary",), vmem_limit_bytes=VMEM_LIMIT,
                                             disable_bounds_checks=True),
        name="moe_combine",
    )(pos, yb, x_all, route, mod, final_g.reshape(1, D_MODEL))


def _routing(route, counts):
    n_tok = route.shape[0]
    n_assign = n_tok * TOP_K
    counts = counts.reshape(N_EXPERTS).astype(jnp.int32)
    padded = (counts + MOE_BLOCK - 1) // MOE_BLOCK * MOE_BLOCK
    end_padded = jnp.cumsum(padded)
    start_padded = end_padded - padded
    n_rows = -(-n_assign // MOE_BLOCK) * MOE_BLOCK + N_EXPERTS * MOE_BLOCK
    n_blocks = n_rows // MOE_BLOCK
    block_start = jnp.arange(n_blocks, dtype=jnp.int32) * MOE_BLOCK
    block_expert = jnp.minimum(jnp.sum((end_padded[None, :] <= block_start[:, None]).astype(jnp.int32), axis=1),
                               N_EXPERTS - 1).astype(jnp.int32)
    nvalid = (end_padded[-1] // MOE_BLOCK).astype(jnp.int32).reshape(1)
    expert = route[:, :TOP_K].astype(jnp.int32)
    rank = route[:, TOP_K:2 * TOP_K].astype(jnp.int32)
    onehot = expert[:, :, None] == jnp.arange(N_EXPERTS, dtype=jnp.int32)[None, None, :]
    pos = rank + jnp.sum(jnp.where(onehot, start_padded[None, None, :], 0), axis=-1)
    return block_expert, nvalid, pos.reshape(-1).astype(jnp.int32), n_rows


def _layer(x_all, mod, li, p, tables, final_g, n_lat, seq, ctx_len, last):
    cos_t, sin_t = tables
    lam_init = 0.8 - 0.6 * math.exp(-0.3 * li)
    pad = jnp.zeros((D_MODEL, P_COLS - C_LORA - 4 * RW_LORA_W - RW_GATE_LORA), F32)
    w_in = p['w_in']
    w_all = jnp.concatenate(
        [w_in[:, 4608:], w_in[:, :4608], p['rw_w1'][0], p['rw_w1'][1], p['rw_a1'][0], p['rw_a1'][1],
         p['rw_g1'], pad], axis=1).astype(BF16)
    p_all = _projection(x_all, mod, p['norm1_g'], w_all, cos_t, sin_t, n_lat, seq)
    o_da = _attention(p_all, p['da_lambda'], p['da_subln_g'], lam_init, n_lat, seq, ctx_len, not last)
    n_tok = n_lat if last else x_all.shape[0]
    o_sc = _short_conv(p_all, p['conv_w'], n_tok, n_lat, seq, ctx_len)
    rw_fwd, rw_bwd = _rwkv_scan(p_all, p, n_lat, seq, ctx_len)
    x_mid, h2, route, counts = _merge(x_all, mod, p_all, o_da, o_sc, rw_fwd, rw_bwd, p, n_tok, n_lat, seq)
    block_expert, nvalid, pos, n_rows = _routing(route, counts)
    xb = _dispatch(h2, pos, n_rows)
    yb = _expert_ffn(xb, block_expert, nvalid, li, p['exp_w1'], p['exp_b1'], p['exp_w2'], p['exp_b2'])
    tm = 256
    pos_by_slot = pos.reshape(n_tok // tm, tm, TOP_K).transpose(0, 2, 1).reshape(-1)
    return _combine(yb, pos_by_slot, x_mid, route, mod, final_g, n_tok, n_lat, seq, last)


def kernel(x, c, ctx, c_ctx, w_mod, b_mod, norm1_g, norm2_g, w_in, da_lambda, da_subln_g, conv_w, rw_w0, rw_w1, rw_w2, rw_a0, rw_a1, rw_a2, rw_g1, rw_g2, rw_k_k, rw_k_a, rw_r_k, rw_lnx_g, rw_lnx_b, w_branch, w_out, router_w, router_b, exp_w1, exp_b1, exp_w2, exp_b2, final_g):
    n_batch, seq, d = x.shape
    ctx_len = ctx.shape[1]
    depth = w_in.shape[0]
    n_lat = n_batch * seq
    assert d == D_MODEL and n_batch <= 8
    proj_tm = max(tm for tm in PROJ_TMS if seq % tm == 0 and (n_batch * ctx_len) % tm == 0)
    assert seq % ctx_len == 0 and ctx_len == 256
    x_all = jnp.concatenate([x.reshape(n_lat, d), ctx.reshape(n_batch * ctx_len, d)], axis=0)
    c_all = jnp.zeros((16, d), F32).at[:n_batch].set(c).at[n_batch].set(c_ctx)
    tables = _rope_tables(seq, proj_tm)
    for li in range(depth):
        p = {
            'norm1_g': norm1_g[li], 'norm2_g': norm2_g[li], 'w_in': w_in[li], 'da_lambda': da_lambda[li],
            'da_subln_g': da_subln_g[li], 'conv_w': conv_w[li],
            'rw_w0': rw_w0[li], 'rw_w1': rw_w1[li], 'rw_w2': rw_w2[li],
            'rw_a0': rw_a0[li], 'rw_a1': rw_a1[li], 'rw_a2': rw_a2[li],
            'rw_g1': rw_g1[li], 'rw_g2': rw_g2[li], 'rw_k_k': rw_k_k[li], 'rw_k_a': rw_k_a[li],
            'rw_r_k': rw_r_k[li], 'rw_lnx_g': rw_lnx_g[li], 'rw_lnx_b': rw_lnx_b[li],
            'w_branch': w_branch[li], 'w_out': w_out[li], 'router_w': router_w[li], 'router_b': router_b[li],
            'exp_w1': exp_w1, 'exp_b1': exp_b1, 'exp_w2': exp_w2, 'exp_b2': exp_b2,
        }
        mod = _modulation(c_all, w_mod[li], b_mod[li]).reshape(16, N_MOD, D_MODEL)
        x_all = _layer(x_all, mod, li, p, tables, final_g, n_lat, seq, ctx_len, li == depth - 1)
    return x_all.reshape(n_batch, seq, d)
```

```python
import functools
import math

import jax
import jax.numpy as jnp
from jax import lax
from jax.experimental import pallas as pl
from jax.experimental.pallas import tpu as pltpu

F32 = jnp.float32
BF16 = jnp.bfloat16

D_MODEL = 1024
GRID_W = 64
NORM_EPS = 1e-6
N_MOD = 6

DA_HEADS = 4
DA_HEAD_DIM = 64
DA_V_DIM = 128
DA_SUBLN_EPS = 1e-5
ROPE_BASE = 10000.0

SC_WIDTH = 512

RW_HEADS = 8
RW_HEAD_DIM = 64
RW_WIDTH = 512
RW_DECAY_SCALE = 0.606531
RW_GN_EPS = 64e-5
RW_CHUNK = 64
RW_LORA_W = 64
RW_GATE_LORA = 160
RW_GATE_PAD = 256

N_EXPERTS = 32
TOP_K = 4
D_FF = 1024
SWIGLU_LIMIT = 7.0
SWIGLU_ALPHA = 1.702
MOE_BLOCK = 512

C_GL = 0
C_Q = 3072
C_K = 3584
C_V = 4096
C_SB = 4608
C_SG = 5120
C_SX = 5632
C_RR = 6144
C_RK = 6656
C_RV = 7168
C_LORA = 7680
P_COLS = 8192

PROJ_TMS = (512, 1024)
PROJ_TN = 1024
ROPE_TILE = C_Q // PROJ_TN

VMEM_LIMIT = 56 * 1024 * 1024
DMA_UNROLL = 8


def _cparams(n_axes):
    return pltpu.CompilerParams(dimension_semantics=("arbitrary",) * n_axes, vmem_limit_bytes=VMEM_LIMIT)


def _sigmoid(x):
    return 1.0 / (1.0 + jnp.exp(-x))


def _nt_dot(a, b):
    return lax.dot_general(a, b, (((1,), (1,)), ((), ())), preferred_element_type=F32)


def _tn_dot(a, b):
    return lax.dot_general(a, b, (((0,), (0,)), ((), ())), preferred_element_type=F32)


def _mod_kernel(c_ref, w_ref, b_ref, o_ref):
    c = c_ref[...]
    s = c * _sigmoid(c)
    o_ref[...] = jnp.dot(s.astype(BF16), w_ref[...].astype(BF16), preferred_element_type=F32) + b_ref[...]


def _modulation(c_all, w_mod, b_mod):
    n = N_MOD * D_MODEL
    tn = 1536
    return pl.pallas_call(
        _mod_kernel,
        grid=(n // tn,),
        in_specs=[pl.BlockSpec((16, D_MODEL), lambda j: (0, 0)),
                  pl.BlockSpec((D_MODEL, tn), lambda j: (0, j)),
                  pl.BlockSpec((1, tn), lambda j: (0, j))],
        out_specs=pl.BlockSpec((16, tn), lambda j: (0, j)),
        out_shape=jax.ShapeDtypeStruct((16, n), F32),
        compiler_params=_cparams(1),
        name="modulation",
    )(c_all, w_mod, b_mod.reshape(1, n))


def _modnorm(x, g, shift, scale):
    ms = jnp.mean(x * x, axis=-1, keepdims=True)
    return (x * lax.rsqrt(ms + NORM_EPS) * g) * (1.0 + scale) + shift


def _proj_kernel(x_ref, mod_ref, g_ref, w_ref, cos_ref, sin_ref, o_ref, h_ref):
    j = pl.program_id(1)

    @pl.when(j == 0)
    def _():
        h = _modnorm(x_ref[...], g_ref[...], mod_ref[0, 0:1, :], mod_ref[0, 1:2, :])
        h_ref[...] = h.astype(BF16)

    @pl.when(j == ROPE_TILE)
    def _():
        acc = jnp.dot(h_ref[...], w_ref[...], preferred_element_type=F32)
        width = acc.shape[1]
        lane = lax.broadcasted_iota(jnp.int32, acc.shape, 1)
        fwd = pltpu.roll(acc, width - 16, axis=1)
        bwd = pltpu.roll(acc, 16, axis=1)
        rot = jnp.where((lane % 32) < 16, fwd, bwd)
        o_ref[...] = acc * cos_ref[...] + rot * sin_ref[...]

    @pl.when(j != ROPE_TILE)
    def _():
        o_ref[...] = jnp.dot(h_ref[...], w_ref[...], preferred_element_type=F32)


def _projection(x_all, mod, g, w_all, cos_t, sin_t, n_lat, seq):
    n = x_all.shape[0]
    tm, tn = cos_t.shape[0] - seq, PROJ_TN
    n_lat_tiles = n_lat // tm
    tiles_per_seq = seq // tm
    n_batch = n_lat // seq

    def seg(i):
        return jnp.where(i < n_lat_tiles, i // tiles_per_seq, n_batch)

    def rope_blk(i):
        return jnp.where(i < n_lat_tiles, i % tiles_per_seq, tiles_per_seq)

    return pl.pallas_call(
        _proj_kernel,
        grid=(n // tm, P_COLS // tn),
        in_specs=[pl.BlockSpec((tm, D_MODEL), lambda i, j: (i, 0)),
                  pl.BlockSpec((1, N_MOD, D_MODEL), lambda i, j: (seg(i), 0, 0)),
                  pl.BlockSpec((1, D_MODEL), lambda i, j: (0, 0)),
                  pl.BlockSpec((D_MODEL, tn), lambda i, j: (0, j)),
                  pl.BlockSpec((tm, tn), lambda i, j: (rope_blk(i), 0)),
                  pl.BlockSpec((tm, tn), lambda i, j: (rope_blk(i), 0))],
        out_specs=pl.BlockSpec((tm, tn), lambda i, j: (i, j)),
        out_shape=jax.ShapeDtypeStruct((n, P_COLS), F32),
        scratch_shapes=[pltpu.VMEM((tm, D_MODEL), BF16)],
        compiler_params=_cparams(2),
        name="norm_projection",
    )(x_all, mod, g.reshape(1, D_MODEL), w_all, cos_t, sin_t)


def _rope_tables(seq, tm):
    half = 16
    pos = jnp.arange(seq, dtype=jnp.int32)
    rows = (pos // GRID_W).astype(F32)
    cols = (pos % GRID_W).astype(F32)
    inv = jnp.power(ROPE_BASE, -jnp.arange(half, dtype=F32) / half)
    ar = rows[:, None] * inv[None, :]
    ac = cols[:, None] * inv[None, :]
    cos64 = jnp.concatenate([jnp.cos(ar), jnp.cos(ar), jnp.cos(ac), jnp.cos(ac)], axis=-1)
    sin64 = jnp.concatenate([-jnp.sin(ar), jnp.sin(ar), -jnp.sin(ac), jnp.sin(ac)], axis=-1)
    reps = 512 // 64
    cos512 = jnp.tile(cos64, (1, reps))
    sin512 = jnp.tile(sin64, (1, reps))
    qs = DA_HEAD_DIM ** -0.5
    cos_l = jnp.concatenate([cos512 * qs, cos512], axis=-1)
    sin_l = jnp.concatenate([sin512 * qs, sin512], axis=-1)
    cos_c = jnp.concatenate([jnp.full((tm, 512), qs, F32), jnp.ones((tm, 512), F32)], axis=-1)
    sin_c = jnp.zeros((tm, 1024), F32)
    return jnp.concatenate([cos_l, cos_c], axis=0), jnp.concatenate([sin_l, sin_c], axis=0)


def _attn_kernel(lam_ref, g_ref, q_ref, kl_ref, vl_ref, kc_ref, vc_ref, o_ref, *, lam_init, n_lat_tiles):
    i = pl.program_id(2)

    @pl.when(i < n_lat_tiles)
    def _():
        _attend(lam_ref, g_ref, q_ref, (kl_ref, vl_ref, kc_ref, vc_ref), o_ref, lam_init)

    @pl.when(i >= n_lat_tiles)
    def _():
        _attend(lam_ref, g_ref, q_ref, (kc_ref, vc_ref), o_ref, lam_init)


def _attend(lam_ref, g_ref, q_ref, kv_refs, o_ref, lam_init):
    n_seg = len(kv_refs) // 2
    q = q_ref[...]
    lane = lax.broadcasted_iota(jnp.int32, q.shape, 1)
    q1 = jnp.where(lane < DA_HEAD_DIM, q, 0.0).astype(BF16)
    q2 = jnp.where(lane >= DA_HEAD_DIM, q, 0.0).astype(BF16)
    ks = [kv_refs[2 * s][...].astype(BF16) for s in range(n_seg)]
    vs = [kv_refs[2 * s + 1][...].astype(BF16) for s in range(n_seg)]

    def softmax_v(qm):
        ss = [_nt_dot(qm, k) for k in ks]
        m = jnp.max(ss[0], axis=-1, keepdims=True)
        for s in ss[1:]:
            m = jnp.maximum(m, jnp.max(s, axis=-1, keepdims=True))
        es = [jnp.exp(s - m) for s in ss]
        l = jnp.sum(es[0], axis=-1, keepdims=True)
        for e in es[1:]:
            l = l + jnp.sum(e, axis=-1, keepdims=True)
        o = jnp.dot(es[0].astype(BF16), vs[0], preferred_element_type=F32)
        for e, v in zip(es[1:], vs[1:]):
            o = o + jnp.dot(e.astype(BF16), v, preferred_element_type=F32)
        return o / l

    lv = lam_ref[...]
    lam = (jnp.exp(jnp.sum(lv[0:1] * lv[1:2], axis=-1, keepdims=True))
           - jnp.exp(jnp.sum(lv[2:3] * lv[3:4], axis=-1, keepdims=True)) + lam_init)
    o = softmax_v(q1) - lam * softmax_v(q2)
    ms = jnp.mean(o * o, axis=-1, keepdims=True)
    o_ref[...] = (o * lax.rsqrt(ms + DA_SUBLN_EPS) * g_ref[...]) * (1.0 - lam_init)


def _attention(p_all, da_lambda, subln_g, lam_init, n_lat, seq, ctx_len, with_ctx_queries):
    n = p_all.shape[0]
    n_batch = n_lat // seq
    hw = DA_V_DIM
    cq, ck, cv = C_Q // hw, C_K // hw, C_V // hw
    tq = ctx_len
    ctx_blk0 = n_lat // ctx_len
    qpb = seq // tq
    n_out = n if with_ctx_queries else n_lat

    def q_blk(b, i):
        return jnp.where(i < qpb, b * qpb + i, ctx_blk0 + b)

    return pl.pallas_call(
        functools.partial(_attn_kernel, lam_init=lam_init, n_lat_tiles=qpb),
        grid=(n_batch, DA_HEADS, qpb + (1 if with_ctx_queries else 0)),
        in_specs=[pl.BlockSpec((4, DA_HEAD_DIM), lambda b, h, i: (0, 0)),
                  pl.BlockSpec((1, hw), lambda b, h, i: (0, 0)),
                  pl.BlockSpec((tq, hw), lambda b, h, i: (q_blk(b, i), cq + h)),
                  pl.BlockSpec((seq, hw), lambda b, h, i: (b, ck + h)),
                  pl.BlockSpec((seq, hw), lambda b, h, i: (b, cv + h)),
                  pl.BlockSpec((ctx_len, hw), lambda b, h, i: (ctx_blk0 + b, ck + h)),
                  pl.BlockSpec((ctx_len, hw), lambda b, h, i: (ctx_blk0 + b, cv + h))],
        out_specs=pl.BlockSpec((tq, hw), lambda b, h, i: (q_blk(b, i), h)),
        out_shape=jax.ShapeDtypeStruct((n_out, DA_HEADS * hw), F32),
        compiler_params=_cparams(3),
        name="diff_attention",
    )(da_lambda, subln_g.reshape(1, hw), p_all, p_all, p_all, p_all, p_all)


def _conv_kernel(sb_ref, sg_ref, sx_ref, gp_ref, xp_ref, gn_ref, xn_ref, w_ref, o_ref, *, n_lat_tiles, tiles_per_seq):
    i = pl.program_id(0)
    starts = (i >= n_lat_tiles) | (i % tiles_per_seq == 0)
    ends = (i >= n_lat_tiles) | (i % tiles_per_seq == tiles_per_seq - 1)
    u = sg_ref[...] * sx_ref[...]
    length = u.shape[0]
    halo_prev = jnp.where(starts, 0.0, gp_ref[7:8, :] * xp_ref[7:8, :])
    halo_next = jnp.where(ends, 0.0, gn_ref[0:1, :] * xn_ref[0:1, :])
    t = lax.broadcasted_iota(jnp.int32, u.shape, 0)
    prev = jnp.where(t == 0, halo_prev, pltpu.roll(u, 1, axis=0))
    nxt = jnp.where(t == length - 1, halo_next, pltpu.roll(u, length - 1, axis=0))
    w = w_ref[...]
    o_ref[...] = sb_ref[...] * (w[0:1] * prev + w[1:2] * u + w[2:3] * nxt)


def _short_conv(p_all, conv_w, n_out, n_lat, seq, ctx_len):
    n = p_all.shape[0]
    tm = ctx_len
    tc = 256
    ncol = SC_WIDTH // tc
    cb, cg, cx = C_SB // tc, C_SG // tc, C_SX // tc
    sub = tm // 8
    last8 = n // 8 - 1

    def main(col):
        return pl.BlockSpec((tm, tc), lambda i, j: (i, col + j))

    def halo_prev(col):
        return pl.BlockSpec((8, tc), lambda i, j: (jnp.maximum(i * sub - 1, 0), col + j))

    def halo_next(col):
        return pl.BlockSpec((8, tc), lambda i, j: (jnp.minimum((i + 1) * sub, last8), col + j))

    return pl.pallas_call(
        functools.partial(_conv_kernel, n_lat_tiles=n_lat // tm, tiles_per_seq=seq // tm),
        grid=(n_out // tm, ncol),
        in_specs=[main(cb), main(cg), main(cx), halo_prev(cg), halo_prev(cx), halo_next(cg), halo_next(cx),
                  pl.BlockSpec((3, tc), lambda i, j: (0, j))],
        out_specs=pl.BlockSpec((tm, tc), lambda i, j: (i, j)),
        out_shape=jax.ShapeDtypeStruct((n_out, SC_WIDTH), F32),
        compiler_params=_cparams(2),
        name="short_conv",
    )(p_all, p_all, p_all, p_all, p_all, p_all, p_all, conv_w)


def _bdot(a, b):
    return jnp.dot(a.astype(BF16), b.astype(BF16), preferred_element_type=F32)


def _block_unit_inverses(lmats, xor_idx):
    eye = jnp.where(xor_idx == 0, 1.0, 0.0)
    lds = [jnp.where(xor_idx < 8, l, 0.0).astype(BF16) for l in lmats]
    ld2s = [jnp.dot(ld, ld, preferred_element_type=F32).astype(BF16) for ld in lds]
    xs = [eye + ld.astype(F32) for ld in lds]
    xs = [x + jnp.dot(x.astype(BF16), ld2, preferred_element_type=F32) for x, ld2 in zip(xs, ld2s)]
    ld4s = [jnp.dot(ld2, ld2, preferred_element_type=F32).astype(BF16) for ld2 in ld2s]
    xs = [x + jnp.dot(x.astype(BF16), ld4, preferred_element_type=F32) for x, ld4 in zip(xs, ld4s)]
    b = 8
    while b < RW_CHUNK:
        offs = [jnp.where(xor_idx >= b, jnp.where(xor_idx < 2 * b, l, 0.0), 0.0).astype(BF16) for l in lmats]
        xbs = [x.astype(BF16) for x in xs]
        ts = [jnp.dot(xb, off, preferred_element_type=F32).astype(BF16) for xb, off in zip(xbs, offs)]
        xs = [x + jnp.dot(t, xb, preferred_element_type=F32) for x, t, xb in zip(xs, ts, xbs)]
        b *= 2
    return xs


def _rwkv_prepare(r, k, v, lo, w0, w2, a0, a2, k_k, k_a, r_k, gsum, direction):
    n = RW_CHUNK
    reverse = direction == 1
    dw = lo[:, direction * RW_LORA_W:(direction + 1) * RW_LORA_W]
    da = lo[:, (2 + direction) * RW_LORA_W:(3 + direction) * RW_LORA_W]
    wl = w0 + jnp.dot(jnp.tanh(dw).astype(BF16), w2.astype(BF16), preferred_element_type=F32)
    logw = -RW_DECAY_SCALE * _sigmoid(wl)
    alpha = _sigmoid(a0 + jnp.dot(da.astype(BF16), a2.astype(BF16), preferred_element_type=F32))
    keff = k * (1.0 + (alpha - 1.0) * k_a)
    kkraw = k * k_k
    kk = kkraw * lax.rsqrt(jnp.maximum(_split_dot(kkraw * kkraw, gsum), 1e-24))
    bonus = _split_dot(r * keff * r_k, gsum) * v
    ti = lax.broadcasted_iota(jnp.int32, (n, n), 0)
    si = lax.broadcasted_iota(jnp.int32, (n, n), 1)
    upto_n = (si >= ti) if reverse else (si <= ti)
    cl = jnp.dot(upto_n.astype(F32), logw, preferred_element_type=F32, precision=lax.Precision.HIGHEST)
    p_in = jnp.exp(cl)
    p_inv = jnp.exp(-cl)
    last = 0 if reverse else n - 1
    p_end = p_in[last:last + 1, :]
    b_t = kk * alpha * p_inv
    k_t = keff * p_inv
    return dict(v=v, a=-kk * jnp.exp(cl - logw), b=b_t, k=k_t, r=r * p_in, b_end=b_t * p_end, k_end=k_t * p_end,
                p_end=p_end, bonus=bonus)


def _rwkv_kernel(tok0_ref, tok1_ref, w0_ref, w2_ref, a0_ref, a2_ref, kk_ref, ka_ref, rkp_ref, gsum_ref,
                 out0_ref, out1_ref, s_ref):
    c = pl.program_id(1)

    @pl.when(c == 0)
    def _():
        s_ref[...] = jnp.zeros_like(s_ref)

    n = RW_CHUNK
    hd = RW_HEAD_DIM
    cw = RW_WIDTH
    tok_refs = (tok0_ref, tok1_ref)
    out_refs = (out0_ref, out1_ref)
    gsum = gsum_ref[...]
    prep = []
    for d in range(2):
        rr, rk, rv, lo = [tok_refs[d][:, j * cw:(j + 1) * cw] for j in range(4)]
        prep.append(_rwkv_prepare(rr, rk, rv, lo, w0_ref[d], w2_ref[d], a0_ref[d], a2_ref[d],
                                  kk_ref[...], ka_ref[...], rkp_ref[...], gsum, d))
        out_refs[d][:, cw:] = prep[d]['bonus']

    m2 = 2 * n
    t2 = lax.broadcasted_iota(jnp.int32, (m2, m2), 0)
    s2 = lax.broadcasted_iota(jnp.int32, (m2, m2), 1)
    xor_idx = t2 ^ s2
    same_head = xor_idx < n
    before = (same_head & (s2 < t2), same_head & (s2 > t2))
    upto = (same_head & (s2 <= t2), same_head & (s2 >= t2))
    first_head = lax.broadcasted_iota(jnp.int32, (n, m2), 1) < hd

    def stack(x):
        return jnp.concatenate([jnp.where(first_head, x, 0.0), jnp.where(first_head, 0.0, x)], axis=0)

    n_pairs = RW_HEADS // 2
    units = [(d, pr) for d in range(2) for pr in range(n_pairs)]

    def part(name, d, pr):
        return prep[d][name][:, pr * m2:(pr + 1) * m2]

    vss = [stack(part('v', d, pr)) for d, pr in units]
    lhss = [jnp.concatenate([stack(part('a', d, pr)), stack(part('r', d, pr))], axis=0).astype(BF16)
            for d, pr in units]
    rhss = [jnp.concatenate([stack(part('b', d, pr)), stack(part('k', d, pr))], axis=0).astype(BF16)
            for d, pr in units]
    grams = [_nt_dot(lhs, rhs) for lhs, rhs in zip(lhss, rhss)]
    s0s = [s_ref[d, pr] for d, pr in units]
    on_states = [_nt_dot(lhs, s0.astype(BF16)) for lhs, s0 in zip(lhss, s0s)]
    l_abs = [jnp.where(before[d], g[:m2, :m2], 0.0) for (d, _), g in zip(units, grams)]
    l_aks = [jnp.where(before[d], g[:m2, m2:], 0.0).astype(BF16) for (d, _), g in zip(units, grams)]
    m_alls = [jnp.concatenate([jnp.where(upto[d], g[m2:, :m2], 0.0), jnp.where(upto[d], g[m2:, m2:], 0.0)],
                              axis=1).astype(BF16) for (d, _), g in zip(units, grams)]
    u_rhss = [os_[:m2] + jnp.dot(l_ak, vs.astype(BF16), preferred_element_type=F32)
              for os_, l_ak, vs in zip(on_states, l_aks, vss)]
    tinvs = _block_unit_inverses(l_abs, xor_idx)
    us = [_bdot(tinv, u_rhs) for tinv, u_rhs in zip(tinvs, u_rhss)]
    uvs = [jnp.concatenate([u, vs], axis=0).astype(BF16) for u, vs in zip(us, vss)]
    y2s = [os_[m2:] + jnp.dot(m_all, uv, preferred_element_type=F32) for os_, m_all, uv in zip(on_states, m_alls, uvs)]
    bks = [jnp.concatenate([stack(part('b_end', d, pr)), stack(part('k_end', d, pr))], axis=0).astype(BF16)
           for d, pr in units]
    for i, (d, pr) in enumerate(units):
        s_ref[d, pr] = s0s[i] * part('p_end', d, pr) + _tn_dot(uvs[i], bks[i])
    for d in range(2):
        out_refs[d][:, :cw] = jnp.concatenate(
            [y2s[d * n_pairs + pr][:n] + y2s[d * n_pairs + pr][n:] for pr in range(n_pairs)], axis=-1)


def _rwkv_scan(p_all, p, n_lat, seq, ctx_len):
    n = p_all.shape[0]
    n_batch = n_lat // seq
    ch = RW_CHUNK
    ctx_c = ctx_len // ch
    seq_c = seq // ch
    lat0 = n_lat // ch
    cw = RW_WIDTH

    def blk_fwd(b, c):
        return jnp.where(c < ctx_c, lat0 + b * ctx_c + c, b * seq_c + (c - ctx_c))

    def blk_bwd(b, c):
        return jnp.where(c < ctx_c, lat0 + b * ctx_c + (ctx_c - 1 - c), b * seq_c + (seq_c - 1 - (c - ctx_c)))

    tok_w = 4 * cw
    assert (C_RK, C_RV, C_LORA) == (C_RR + cw, C_RR + 2 * cw, C_RR + 3 * cw) and C_RR % tok_w == 0

    def tok_spec(blk):
        return pl.BlockSpec((ch, tok_w), lambda b, c: (blk(b, c), C_RR // tok_w))

    def par_spec(shape):
        nd = len(shape)
        return pl.BlockSpec(shape, lambda b, c: (0,) * nd)

    def out_spec(blk):
        return pl.BlockSpec((ch, 2 * cw), lambda b, c: (blk(b, c), 0))

    sds = jax.ShapeDtypeStruct((n, 2 * cw), F32)
    return pl.pallas_call(
        _rwkv_kernel,
        grid=(n_batch, ctx_c + seq_c),
        in_specs=[tok_spec(blk_fwd), tok_spec(blk_bwd),
                  par_spec((2, 1, cw)), par_spec((2, RW_LORA_W, cw)), par_spec((2, 1, cw)), par_spec((2, RW_LORA_W, cw)),
                  par_spec((1, cw)), par_spec((1, cw)), par_spec((1, cw)), par_spec((cw, cw))],
        out_specs=[out_spec(blk_fwd), out_spec(blk_bwd)],
        out_shape=[sds, sds],
        scratch_shapes=[pltpu.VMEM((2, RW_HEADS // 2, 2 * RW_HEAD_DIM, 2 * RW_HEAD_DIM), F32)],
        compiler_params=_cparams(2),
        name="rwkv_scan",
    )(p_all, p_all,
      p['rw_w0'].reshape(2, 1, cw), p['rw_w2'], p['rw_a0'].reshape(2, 1, cw), p['rw_a2'],
      p['rw_k_k'].reshape(1, cw), p['rw_k_a'].reshape(1, cw), p['rw_r_k'].reshape(1, cw), _head_group_ones())


def _head_group_ones():
    head_of = jnp.arange(RW_WIDTH) // RW_HEAD_DIM
    return (head_of[:, None] == head_of[None, :]).astype(BF16)


def _split_dot(x, w_bf16):
    hi = x.astype(BF16)
    lo = (x - hi.astype(F32)).astype(BF16)
    return (jnp.dot(hi, w_bf16, preferred_element_type=F32) + jnp.dot(lo, w_bf16, preferred_element_type=F32))


def _merge_kernel(x_ref, mod_ref, gl_ref, oda_ref, osc_ref, rw0_ref, rw1_ref, lo_ref,
                  g2_ref, lng_ref, lnb_ref, gavg_ref, wb_ref, wo_ref, n2g_ref, rw_ref, rb_ref,
                  xo_ref, h2_ref, route_ref, cnt_ref):
    @pl.when(pl.program_id(0) == 0)
    def _():
        cnt_ref[...] = jnp.zeros_like(cnt_ref)

    ysum = rw0_ref[:, :RW_WIDTH] + rw1_ref[:, :RW_WIDTH]
    gavg = gavg_ref[...]
    mu = _split_dot(ysum, gavg)
    yc = ysum - mu
    var = _split_dot(yc * yc, gavg)
    yn = yc * lax.rsqrt(var + RW_GN_EPS) * lng_ref[...] + lnb_ref[...]
    dg = lo_ref[:, 4 * RW_LORA_W:4 * RW_LORA_W + RW_GATE_PAD]
    gate = jnp.dot(_sigmoid(dg).astype(BF16), g2_ref[...], preferred_element_type=F32)
    o_rw = (yn + rw0_ref[:, RW_WIDTH:] + rw1_ref[:, RW_WIDTH:]) * gate

    branches = (oda_ref[...], osc_ref[...], o_rw)
    merged = None
    for i, br in enumerate(branches):
        proj = jnp.dot(br.astype(BF16), wb_ref[i], preferred_element_type=F32)
        term = _sigmoid(gl_ref[:, i * D_MODEL:(i + 1) * D_MODEL]) * proj
        merged = term if merged is None else merged + term
    y = jnp.dot(merged.astype(BF16), wo_ref[...], preferred_element_type=F32)
    x_new = x_ref[...] + mod_ref[0, 2:3, :] * y
    xo_ref[...] = x_new
    h2 = _modnorm(x_new, n2g_ref[...], mod_ref[0, 3:4, :], mod_ref[0, 4:5, :])
    h2_ref[...] = h2
    logits = (jnp.dot(h2, rw_ref[...], preferred_element_type=F32, precision=lax.Precision.HIGHEST)
              + rb_ref[...])
    _route(logits, route_ref, cnt_ref)


def _route(logits, route_ref, cnt_ref):
    tm, ne = logits.shape
    lane = lax.broadcasted_iota(jnp.int32, logits.shape, 1)
    cur = logits
    vals, onehots, ids = [], [], []
    for _ in range(TOP_K):
        m = jnp.max(cur, axis=-1, keepdims=True)
        idx = jnp.min(jnp.where(cur == m, lane, ne), axis=-1, keepdims=True)
        hit = lane == idx
        vals.append(m)
        ids.append(idx.astype(F32))
        onehots.append(jnp.where(hit, 1.0, 0.0))
        cur = jnp.where(hit, -jnp.inf, cur)
    es = [jnp.exp(v - vals[0]) for v in vals]
    denom = es[0]
    for e in es[1:]:
        denom = denom + e
    gates = [e / denom for e in es]
    ti = lax.broadcasted_iota(jnp.int32, (tm, tm), 0)
    si = lax.broadcasted_iota(jnp.int32, (tm, tm), 1)
    earlier = jnp.where(si < ti, 1.0, 0.0).astype(BF16)
    base = cnt_ref[...]
    ranks = []
    for oh in onehots:
        prefix = jnp.dot(earlier, oh.astype(BF16), preferred_element_type=F32)
        ranks.append(jnp.sum(oh * (base + prefix), axis=-1, keepdims=True))
        base = base + jnp.sum(oh, axis=0, keepdims=True)
    cnt_ref[...] = base
    out_lane = lax.broadcasted_iota(jnp.int32, route_ref.shape, 1)
    out = jnp.zeros(route_ref.shape, F32)
    for j, col in enumerate(ids + ranks + gates):
        out = jnp.where(out_lane == j, col, out)
    route_ref[...] = out


def _merge(x_all, mod, p_all, o_da, o_sc, rw_fwd, rw_bwd, p, n_out, n_lat, seq):
    n = n_out
    tm = 256
    n_lat_tiles = n_lat // tm
    tiles_per_seq = seq // tm
    n_batch = n_lat // seq

    def seg(i):
        return jnp.where(i < n_lat_tiles, i // tiles_per_seq, n_batch)

    def row(w):
        return pl.BlockSpec((tm, w), lambda i: (i, 0))

    def const(shape):
        nd = len(shape)
        return pl.BlockSpec(shape, lambda i: (0,) * nd)

    g2 = jnp.zeros((RW_GATE_PAD, RW_WIDTH), F32).at[:RW_GATE_LORA].set(p['rw_g2']).astype(BF16)
    head_of = jnp.arange(RW_WIDTH) // RW_HEAD_DIM
    gavg = (head_of[:, None] == head_of[None, :]).astype(F32) / RW_HEAD_DIM
    return pl.pallas_call(
        _merge_kernel,
        grid=(n // tm,),
        in_specs=[row(D_MODEL),
                  pl.BlockSpec((1, N_MOD, D_MODEL), lambda i: (seg(i), 0, 0)),
                  pl.BlockSpec((tm, 3 * D_MODEL), lambda i: (i, C_GL // (3 * D_MODEL))),
                  row(512), row(512), row(2 * RW_WIDTH), row(2 * RW_WIDTH),
                  pl.BlockSpec((tm, 512), lambda i: (i, C_LORA // 512)),
                  const((RW_GATE_PAD, RW_WIDTH)), const((1, RW_WIDTH)), const((1, RW_WIDTH)),
                  const((RW_WIDTH, RW_WIDTH)),
                  const((3, 512, D_MODEL)), const((D_MODEL, D_MODEL)), const((1, D_MODEL)),
                  const((D_MODEL, N_EXPERTS)), const((1, N_EXPERTS))],
        out_specs=[row(D_MODEL),
                   row(D_MODEL),
                   row(128),
                   const((1, N_EXPERTS))],
        out_shape=[jax.ShapeDtypeStruct((n, D_MODEL), F32),
                   jax.ShapeDtypeStruct((n, D_MODEL), F32),
                   jax.ShapeDtypeStruct((n, 128), F32),
                   jax.ShapeDtypeStruct((1, N_EXPERTS), F32)],
        compiler_params=_cparams(1),
        name="merge_router",
    )(x_all, mod, p_all, o_da, o_sc, rw_fwd, rw_bwd, p_all,
      g2, p['rw_lnx_g'].reshape(1, RW_WIDTH), p['rw_lnx_b'].reshape(1, RW_WIDTH), gavg.astype(BF16),
      p['w_branch'].astype(BF16), p['w_out'].astype(BF16), p['norm2_g'].reshape(1, D_MODEL),
      p['router_w'], p['router_b'].reshape(1, N_EXPERTS))


def _dispatch_kernel(pos_ref, h2_ref, xb_in, xb_out, sem, *, tm):
    del xb_in
    i = pl.program_id(0)

    def copy(t, row):
        return pltpu.make_async_copy(h2_ref.at[pl.ds(t, 1), :], xb_out.at[pl.ds(row, 1), :], sem.at[0])

    def issue(it, carry):
        for u in range(DMA_UNROLL // TOP_K):
            t = it * (DMA_UNROLL // TOP_K) + u
            base = (i * tm + t) * TOP_K
            for k in range(TOP_K):
                copy(t, pos_ref[base + k]).start(priority=k % 2)
        return carry

    def drain(it, carry):
        for _ in range(DMA_UNROLL):
            copy(0, 0).wait()
        return carry

    lax.fori_loop(0, tm * TOP_K // DMA_UNROLL, issue, 0)
    lax.fori_loop(0, tm * TOP_K // DMA_UNROLL, drain, 0)


def _dispatch(h2, pos, n_rows):
    n_tok = pos.shape[0] // TOP_K
    tm = 256
    grid_spec = pltpu.PrefetchScalarGridSpec(
        num_scalar_prefetch=1,
        grid=(n_tok // tm,),
        in_specs=[pl.BlockSpec((tm, D_MODEL), lambda i, pos: (i, 0)),
                  pl.BlockSpec(memory_space=pl.ANY)],
        out_specs=pl.BlockSpec(memory_space=pl.ANY),
        scratch_shapes=[pltpu.SemaphoreType.DMA((1,))],
    )
    return pl.pallas_call(
        functools.partial(_dispatch_kernel, tm=tm),
        grid_spec=grid_spec,
        out_shape=jax.ShapeDtypeStruct((n_rows, D_MODEL), F32),
        input_output_aliases={2: 0},
        compiler_params=pltpu.CompilerParams(dimension_semantics=("arbitrary",), vmem_limit_bytes=VMEM_LIMIT,
                                             disable_bounds_checks=True),
        name="moe_dispatch",
    )(pos, h2, jnp.zeros((n_rows, D_MODEL), F32))


def _expert_kernel(be_ref, nvalid_ref, x_ref, w1_ref, b1_ref, w2_ref, b2_ref, o_ref, w1b, w2b):
    i = pl.program_id(0)
    nvalid = nvalid_ref[0]

    @pl.when(i < nvalid)
    def _():
        @pl.when((i == 0) | (be_ref[i] != be_ref[jnp.maximum(i - 1, 0)]))
        def _():
            w1b[...] = w1_ref[0, 0].astype(BF16)
            w2b[...] = w2_ref[0, 0].astype(BF16)

        hid = jnp.dot(x_ref[...].astype(BF16), w1b[...], preferred_element_type=F32) + b1_ref[0, 0]
        glu = jnp.minimum(hid[:, :D_FF], SWIGLU_LIMIT)
        lin = jnp.clip(hid[:, D_FF:], -SWIGLU_LIMIT, SWIGLU_LIMIT)
        act = glu * _sigmoid(SWIGLU_ALPHA * glu) * (lin + 1.0)
        o_ref[...] = jnp.dot(act.astype(BF16), w2b[...], preferred_element_type=F32) + b2_ref[0, 0]

    @pl.when(i >= nvalid)
    def _():
        o_ref[...] = jnp.zeros_like(o_ref)


def _expert_ffn(xb, block_expert, nvalid, li, w1, b1, w2, b2):
    n_rows = xb.shape[0]
    mb = MOE_BLOCK
    depth = w1.shape[0]
    grid_spec = pltpu.PrefetchScalarGridSpec(
        num_scalar_prefetch=2,
        grid=(n_rows // mb,),
        in_specs=[pl.BlockSpec((mb, D_MODEL), lambda i, be, nv: (i, 0)),
                  pl.BlockSpec((1, 1, D_MODEL, 2 * D_FF), lambda i, be, nv: (li, be[i], 0, 0)),
                  pl.BlockSpec((1, 1, 1, 2 * D_FF), lambda i, be, nv: (li, be[i], 0, 0)),
                  pl.BlockSpec((1, 1, D_FF, D_MODEL), lambda i, be, nv: (li, be[i], 0, 0)),
                  pl.BlockSpec((1, 1, 1, D_MODEL), lambda i, be, nv: (li, be[i], 0, 0))],
        out_specs=pl.BlockSpec((mb, D_MODEL), lambda i, be, nv: (i, 0)),
        scratch_shapes=[pltpu.VMEM((D_MODEL, 2 * D_FF), BF16), pltpu.VMEM((D_FF, D_MODEL), BF16)],
    )
    return pl.pallas_call(
        _expert_kernel,
        grid_spec=grid_spec,
        out_shape=jax.ShapeDtypeStruct((n_rows, D_MODEL), F32),
        compiler_params=_cparams(1),
        name="expert_ffn",
    )(block_expert, nvalid, xb, w1, b1.reshape(depth, N_EXPERTS, 1, 2 * D_FF), w2,
      b2.reshape(depth, N_EXPERTS, 1, D_MODEL))


def _combine_kernel(pos_ref, yb_hbm, x_ref, route_ref, mod_ref, fg_ref, o_ref, buf, sem, *, tm, n_tiles, final_norm):
    i = pl.program_id(0)

    def copy(idx, slot, r):
        return pltpu.make_async_copy(yb_hbm.at[pl.ds(idx, 1), :], buf.at[slot, pl.ds(r, 1), :], sem.at[slot])

    def issue(tile, slot):
        def body(it, carry):
            for u in range(DMA_UNROLL):
                r = it * DMA_UNROLL + u
                copy(pos_ref[tile * (tm * TOP_K) + r], slot, r).start(priority=u % 2)
            return carry
        lax.fori_loop(0, tm * TOP_K // DMA_UNROLL, body, 0)

    @pl.when(i == 0)
    def _():
        issue(0, 0)

    @pl.when(i + 1 < n_tiles)
    def _():
        issue(i + 1, (i + 1) % 2)

    slot = i % 2

    def wait_body(it, carry):
        for _ in range(DMA_UNROLL):
            copy(0, slot, 0).wait()
        return carry
    lax.fori_loop(0, tm * TOP_K // DMA_UNROLL, wait_body, 0)

    f = route_ref[:, 2 * TOP_K:2 * TOP_K + 1] * buf[slot, pl.ds(0, tm), :]
    for kk in range(1, TOP_K):
        f = f + route_ref[:, 2 * TOP_K + kk:2 * TOP_K + kk + 1] * buf[slot, pl.ds(kk * tm, tm), :]
    x_new = x_ref[...] + mod_ref[0, 5:6, :] * f
    if final_norm:
        ms = jnp.mean(x_new * x_new, axis=-1, keepdims=True)
        x_new = x_new * lax.rsqrt(ms + NORM_EPS) * fg_ref[...]
    o_ref[...] = x_new


def _combine(yb, pos, x_all, route, mod, final_g, n_out, n_lat, seq, final_norm):
    tm = 256
    n_tiles = n_out // tm
    n_lat_tiles = n_lat // tm
    tiles_per_seq = seq // tm
    n_batch = n_lat // seq

    def seg(i):
        return jnp.where(i < n_lat_tiles, i // tiles_per_seq, n_batch)

    grid_spec = pltpu.PrefetchScalarGridSpec(
        num_scalar_prefetch=1,
        grid=(n_tiles,),
        in_specs=[pl.BlockSpec(memory_space=pl.ANY),
                  pl.BlockSpec((tm, D_MODEL), lambda i, pos: (i, 0)),
                  pl.BlockSpec((tm, 128), lambda i, pos: (i, 0)),
                  pl.BlockSpec((1, N_MOD, D_MODEL), lambda i, pos: (seg(i), 0, 0)),
                  pl.BlockSpec((1, D_MODEL), lambda i, pos: (0, 0))],
        out_specs=pl.BlockSpec((tm, D_MODEL), lambda i, pos: (i, 0)),
        scratch_shapes=[pltpu.VMEM((2, tm * TOP_K, D_MODEL), F32), pltpu.SemaphoreType.DMA((2,))],
    )
    return pl.pallas_call(
        functools.partial(_combine_kernel, tm=tm, n_tiles=n_tiles, final_norm=final_norm),
        grid_spec=grid_spec,
        out_shape=jax.ShapeDtypeStruct((n_out, D_MODEL), F32),
        compiler_params=pltpu.CompilerParams(dimension_semantics=("arbitrary",), vmem_limit_bytes=VMEM_LIMIT,
                                             disable_bounds_checks=True),
        name="moe_combine",
    )(pos, yb, x_all, route, mod, final_g.reshape(1, D_MODEL))


def _routing(route, counts):
    n_tok = route.shape[0]
    n_assign = n_tok * TOP_K
    counts = counts.reshape(N_EXPERTS).astype(jnp.int32)
    padded = (counts + MOE_BLOCK - 1) // MOE_BLOCK * MOE_BLOCK
    end_padded = jnp.cumsum(padded)
    start_padded = end_padded - padded
    n_rows = -(-n_assign // MOE_BLOCK) * MOE_BLOCK + N_EXPERTS * MOE_BLOCK
    n_blocks = n_rows // MOE_BLOCK
    block_start = jnp.arange(n_blocks, dtype=jnp.int32) * MOE_BLOCK
    block_expert = jnp.minimum(jnp.sum((end_padded[None, :] <= block_start[:, None]).astype(jnp.int32), axis=1),
                               N_EXPERTS - 1).astype(jnp.int32)
    nvalid = (end_padded[-1] // MOE_BLOCK).astype(jnp.int32).reshape(1)
    expert = route[:, :TOP_K].astype(jnp.int32)
    rank = route[:, TOP_K:2 * TOP_K].astype(jnp.int32)
    onehot = expert[:, :, None] == jnp.arange(N_EXPERTS, dtype=jnp.int32)[None, None, :]
    pos = rank + jnp.sum(jnp.where(onehot, start_padded[None, None, :], 0), axis=-1)
    return block_expert, nvalid, pos.reshape(-1).astype(jnp.int32), n_rows


def _layer(x_all, mod, li, p, tables, final_g, n_lat, seq, ctx_len, last):
    cos_t, sin_t = tables
    lam_init = 0.8 - 0.6 * math.exp(-0.3 * li)
    pad = jnp.zeros((D_MODEL, P_COLS - C_LORA - 4 * RW_LORA_W - RW_GATE_LORA), F32)
    w_in = p['w_in']
    w_all = jnp.concatenate(
        [w_in[:, 4608:], w_in[:, :4608], p['rw_w1'][0], p['rw_w1'][1], p['rw_a1'][0], p['rw_a1'][1],
         p['rw_g1'], pad], axis=1).astype(BF16)
    p_all = _projection(x_all, mod, p['norm1_g'], w_all, cos_t, sin_t, n_lat, seq)
    o_da = _attention(p_all, p['da_lambda'], p['da_subln_g'], lam_init, n_lat, seq, ctx_len, not last)
    n_tok = n_lat if last else x_all.shape[0]
    o_sc = _short_conv(p_all, p['conv_w'], n_tok, n_lat, seq, ctx_len)
    rw_fwd, rw_bwd = _rwkv_scan(p_all, p, n_lat, seq, ctx_len)
    x_mid, h2, route, counts = _merge(x_all, mod, p_all, o_da, o_sc, rw_fwd, rw_bwd, p, n_tok, n_lat, seq)
    block_expert, nvalid, pos, n_rows = _routing(route, counts)
    xb = _dispatch(h2, pos, n_rows)
    yb = _expert_ffn(xb, block_expert, nvalid, li, p['exp_w1'], p['exp_b1'], p['exp_w2'], p['exp_b2'])
    tm = 256
    pos_by_slot = pos.reshape(n_tok // tm, tm, TOP_K).transpose(0, 2, 1).reshape(-1)
    return _combine(yb, pos_by_slot, x_mid, route, mod, final_g, n_tok, n_lat, seq, last)


def kernel(x, c, ctx, c_ctx, w_mod, b_mod, norm1_g, norm2_g, w_in, da_lambda, da_subln_g, conv_w, rw_w0, rw_w1, rw_w2, rw_a0, rw_a1, rw_a2, rw_g1, rw_g2, rw_k_k, rw_k_a, rw_r_k, rw_lnx_g, rw_lnx_b, w_branch, w_out, router_w, router_b, exp_w1, exp_b1, exp_w2, exp_b2, final_g):
    n_batch, seq, d = x.shape
    ctx_len = ctx.shape[1]
    depth = w_in.shape[0]
    n_lat = n_batch * seq
    assert d == D_MODEL and n_batch <= 8
    proj_tm = max(tm for tm in PROJ_TMS if seq % tm == 0 and (n_batch * ctx_len) % tm == 0)
    assert seq % ctx_len == 0 and ctx_len == 256
    x_all = jnp.concatenate([x.reshape(n_lat, d), ctx.reshape(n_batch * ctx_len, d)], axis=0)
    c_all = jnp.zeros((16, d), F32).at[:n_batch].set(c).at[n_batch].set(c_ctx)
    tables = _rope_tables(seq, proj_tm)
    for li in range(depth):
        p = {
            'norm1_g': norm1_g[li], 'norm2_g': norm2_g[li], 'w_in': w_in[li], 'da_lambda': da_lambda[li],
            'da_subln_g': da_subln_g[li], 'conv_w': conv_w[li],
            'rw_w0': rw_w0[li], 'rw_w1': rw_w1[li], 'rw_w2': rw_w2[li],
            'rw_a0': rw_a0[li], 'rw_a1': rw_a1[li], 'rw_a2': rw_a2[li],
            'rw_g1': rw_g1[li], 'rw_g2': rw_g2[li], 'rw_k_k': rw_k_k[li], 'rw_k_a': rw_k_a[li],
            'rw_r_k': rw_r_k[li], 'rw_lnx_g': rw_lnx_g[li], 'rw_lnx_b': rw_lnx_b[li],
            'w_branch': w_branch[li], 'w_out': w_out[li], 'router_w': router_w[li], 'router_b': router_b[li],
            'exp_w1': exp_w1, 'exp_b1': exp_b1, 'exp_w2': exp_w2, 'exp_b2': exp_b2,
        }
        mod = _modulation(c_all, w_mod[li], b_mod[li]).reshape(16, N_MOD, D_MODEL)
        x_all = _layer(x_all, mod, li, p, tables, final_g, n_lat, seq, ctx_len, li == depth - 1)
    return x_all.reshape(n_batch, seq, d)
```

```python
import functools
import math

import jax
import jax.numpy as jnp
from jax import lax
from jax.experimental import pallas as pl
from jax.experimental.pallas import tpu as pltpu

F32 = jnp.float32
BF16 = jnp.bfloat16

D_MODEL = 1024
GRID_W = 64
NORM_EPS = 1e-6
N_MOD = 6

DA_HEADS = 4
DA_HEAD_DIM = 64
DA_V_DIM = 128
DA_SUBLN_EPS = 1e-5
ROPE_BASE = 10000.0

SC_WIDTH = 512

RW_HEADS = 8
RW_HEAD_DIM = 64
RW_WIDTH = 512
RW_DECAY_SCALE = 0.606531
RW_GN_EPS = 64e-5
RW_CHUNK = 64
RW_LORA_W = 64
RW_GATE_LORA = 160
RW_GATE_PAD = 256

N_EXPERTS = 32
TOP_K = 4
D_FF = 1024
SWIGLU_LIMIT = 7.0
SWIGLU_ALPHA = 1.702
MOE_BLOCK = 512

C_GL = 0
C_Q = 3072
C_K = 3584
C_V = 4096
C_SB = 4608
C_SG = 5120
C_SX = 5632
C_RR = 6144
C_RK = 6656
C_RV = 7168
C_LORA = 7680
P_COLS = 8192

PROJ_TMS = (512, 1024)
PROJ_TN = 1024
ROPE_TILE = C_Q // PROJ_TN

VMEM_LIMIT = 56 * 1024 * 1024
ATTN_KEY_CHUNK = 512
DMA_UNROLL = 8


def _cparams(n_axes):
    return pltpu.CompilerParams(dimension_semantics=("arbitrary",) * n_axes, vmem_limit_bytes=VMEM_LIMIT)


def _sigmoid(x):
    return 1.0 / (1.0 + jnp.exp(-x))


def _nt_dot(a, b):
    return lax.dot_general(a, b, (((1,), (1,)), ((), ())), preferred_element_type=F32)


def _tn_dot(a, b):
    return lax.dot_general(a, b, (((0,), (0,)), ((), ())), preferred_element_type=F32)


def _mod_kernel(c_ref, w_ref, b_ref, o_ref):
    c = c_ref[...]
    s = c * _sigmoid(c)
    o_ref[...] = jnp.dot(s.astype(BF16), w_ref[...].astype(BF16), preferred_element_type=F32) + b_ref[...]


def _modulation(c_all, w_mod, b_mod):
    n = N_MOD * D_MODEL
    tn = 1536
    return pl.pallas_call(
        _mod_kernel,
        grid=(n // tn,),
        in_specs=[pl.BlockSpec((16, D_MODEL), lambda j: (0, 0)),
                  pl.BlockSpec((D_MODEL, tn), lambda j: (0, j)),
                  pl.BlockSpec((1, tn), lambda j: (0, j))],
        out_specs=pl.BlockSpec((16, tn), lambda j: (0, j)),
        out_shape=jax.ShapeDtypeStruct((16, n), F32),
        compiler_params=_cparams(1),
        name="modulation",
    )(c_all, w_mod, b_mod.reshape(1, n))


def _modnorm(x, g, shift, scale):
    ms = jnp.mean(x * x, axis=-1, keepdims=True)
    return (x * lax.rsqrt(ms + NORM_EPS) * g) * (1.0 + scale) + shift


def _proj_kernel(x_ref, mod_ref, g_ref, w_ref, cos_ref, sin_ref, o_ref, h_ref):
    j = pl.program_id(1)

    @pl.when(j == 0)
    def _():
        h = _modnorm(x_ref[...], g_ref[...], mod_ref[0, 0:1, :], mod_ref[0, 1:2, :])
        h_ref[...] = h.astype(BF16)

    @pl.when(j == ROPE_TILE)
    def _():
        acc = jnp.dot(h_ref[...], w_ref[...], preferred_element_type=F32)
        width = acc.shape[1]
        lane = lax.broadcasted_iota(jnp.int32, acc.shape, 1)
        fwd = pltpu.roll(acc, width - 16, axis=1)
        bwd = pltpu.roll(acc, 16, axis=1)
        rot = jnp.where((lane % 32) < 16, fwd, bwd)
        o_ref[...] = acc * cos_ref[...] + rot * sin_ref[...]

    @pl.when(j != ROPE_TILE)
    def _():
        o_ref[...] = jnp.dot(h_ref[...], w_ref[...], preferred_element_type=F32)


def _projection(x_all, mod, g, w_all, cos_t, sin_t, n_lat, seq):
    n = x_all.shape[0]
    tm, tn = cos_t.shape[0] - seq, PROJ_TN
    n_lat_tiles = n_lat // tm
    tiles_per_seq = seq // tm
    n_batch = n_lat // seq

    def seg(i):
        return jnp.where(i < n_lat_tiles, i // tiles_per_seq, n_batch)

    def rope_blk(i):
        return jnp.where(i < n_lat_tiles, i % tiles_per_seq, tiles_per_seq)

    return pl.pallas_call(
        _proj_kernel,
        grid=(n // tm, P_COLS // tn),
        in_specs=[pl.BlockSpec((tm, D_MODEL), lambda i, j: (i, 0)),
                  pl.BlockSpec((1, N_MOD, D_MODEL), lambda i, j: (seg(i), 0, 0)),
                  pl.BlockSpec((1, D_MODEL), lambda i, j: (0, 0)),
                  pl.BlockSpec((D_MODEL, tn), lambda i, j: (0, j)),
                  pl.BlockSpec((tm, tn), lambda i, j: (rope_blk(i), 0)),
                  pl.BlockSpec((tm, tn), lambda i, j: (rope_blk(i), 0))],
        out_specs=pl.BlockSpec((tm, tn), lambda i, j: (i, j)),
        out_shape=jax.ShapeDtypeStruct((n, P_COLS), F32),
        scratch_shapes=[pltpu.VMEM((tm, D_MODEL), BF16)],
        compiler_params=_cparams(2),
        name="norm_projection",
    )(x_all, mod, g.reshape(1, D_MODEL), w_all, cos_t, sin_t)


def _rope_tables(seq, tm):
    half = 16
    pos = jnp.arange(seq, dtype=jnp.int32)
    rows = (pos // GRID_W).astype(F32)
    cols = (pos % GRID_W).astype(F32)
    inv = jnp.power(ROPE_BASE, -jnp.arange(half, dtype=F32) / half)
    ar = rows[:, None] * inv[None, :]
    ac = cols[:, None] * inv[None, :]
    cos64 = jnp.concatenate([jnp.cos(ar), jnp.cos(ar), jnp.cos(ac), jnp.cos(ac)], axis=-1)
    sin64 = jnp.concatenate([-jnp.sin(ar), jnp.sin(ar), -jnp.sin(ac), jnp.sin(ac)], axis=-1)
    reps = 512 // 64
    cos512 = jnp.tile(cos64, (1, reps))
    sin512 = jnp.tile(sin64, (1, reps))
    qs = DA_HEAD_DIM ** -0.5
    cos_l = jnp.concatenate([cos512 * qs, cos512], axis=-1)
    sin_l = jnp.concatenate([sin512 * qs, sin512], axis=-1)
    cos_c = jnp.concatenate([jnp.full((tm, 512), qs, F32), jnp.ones((tm, 512), F32)], axis=-1)
    sin_c = jnp.zeros((tm, 1024), F32)
    return jnp.concatenate([cos_l, cos_c], axis=0), jnp.concatenate([sin_l, sin_c], axis=0)


def _attn_kernel(lam_ref, g_ref, q_ref, kl_ref, vl_ref, kc_ref, vc_ref, o_ref, *, lam_init, n_lat_tiles):
    i = pl.program_id(2)

    @pl.when(i < n_lat_tiles)
    def _():
        _attend(lam_ref, g_ref, q_ref, (kl_ref, vl_ref, kc_ref, vc_ref), o_ref, lam_init)

    @pl.when(i >= n_lat_tiles)
    def _():
        _attend(lam_ref, g_ref, q_ref, (kc_ref, vc_ref), o_ref, lam_init)


def _attend(lam_ref, g_ref, q_ref, kv_refs, o_ref, lam_init):
    n_seg = len(kv_refs) // 2
    q = q_ref[...]
    lane = lax.broadcasted_iota(jnp.int32, q.shape, 1)
    q1 = jnp.where(lane < DA_HEAD_DIM, q, 0.0).astype(BF16)
    q2 = jnp.where(lane >= DA_HEAD_DIM, q, 0.0).astype(BF16)
    chunks = []
    for s in range(n_seg):
        k_ref, v_ref = kv_refs[2 * s], kv_refs[2 * s + 1]
        ck = min(k_ref.shape[0], ATTN_KEY_CHUNK)
        for c in range(k_ref.shape[0] // ck):
            chunks.append((k_ref[c * ck:(c + 1) * ck, :].astype(BF16), v_ref[c * ck:(c + 1) * ck, :].astype(BF16)))

    def softmax_v(qm):
        m = l = acc = None
        for k, v in chunks:
            s = _nt_dot(qm, k)
            m_c = jnp.max(s, axis=-1, keepdims=True)
            if m is None:
                m = m_c
                e = jnp.exp(s - m)
                l = jnp.sum(e, axis=-1, keepdims=True)
                acc = jnp.dot(e.astype(BF16), v, preferred_element_type=F32)
            else:
                m_new = jnp.maximum(m, m_c)
                scale = jnp.exp(m - m_new)
                e = jnp.exp(s - m_new)
                l = scale * l + jnp.sum(e, axis=-1, keepdims=True)
                acc = scale * acc + jnp.dot(e.astype(BF16), v, preferred_element_type=F32)
                m = m_new
        return acc / l

    lv = lam_ref[...]
    lam = (jnp.exp(jnp.sum(lv[0:1] * lv[1:2], axis=-1, keepdims=True))
           - jnp.exp(jnp.sum(lv[2:3] * lv[3:4], axis=-1, keepdims=True)) + lam_init)
    o = softmax_v(q1) - lam * softmax_v(q2)
    ms = jnp.mean(o * o, axis=-1, keepdims=True)
    o_ref[...] = (o * lax.rsqrt(ms + DA_SUBLN_EPS) * g_ref[...]) * (1.0 - lam_init)


def _attention(p_all, da_lambda, subln_g, lam_init, n_lat, seq, ctx_len, with_ctx_queries):
    n = p_all.shape[0]
    n_batch = n_lat // seq
    hw = DA_V_DIM
    cq, ck, cv = C_Q // hw, C_K // hw, C_V // hw
    tq = ctx_len
    ctx_blk0 = n_lat // ctx_len
    qpb = seq // tq
    n_out = n if with_ctx_queries else n_lat

    def q_blk(b, i):
        return jnp.where(i < qpb, b * qpb + i, ctx_blk0 + b)

    return pl.pallas_call(
        functools.partial(_attn_kernel, lam_init=lam_init, n_lat_tiles=qpb),
        grid=(n_batch, DA_HEADS, qpb + (1 if with_ctx_queries else 0)),
        in_specs=[pl.BlockSpec((4, DA_HEAD_DIM), lambda b, h, i: (0, 0)),
                  pl.BlockSpec((1, hw), lambda b, h, i: (0, 0)),
                  pl.BlockSpec((tq, hw), lambda b, h, i: (q_blk(b, i), cq + h)),
                  pl.BlockSpec((seq, hw), lambda b, h, i: (b, ck + h)),
                  pl.BlockSpec((seq, hw), lambda b, h, i: (b, cv + h)),
                  pl.BlockSpec((ctx_len, hw), lambda b, h, i: (ctx_blk0 + b, ck + h)),
                  pl.BlockSpec((ctx_len, hw), lambda b, h, i: (ctx_blk0 + b, cv + h))],
        out_specs=pl.BlockSpec((tq, hw), lambda b, h, i: (q_blk(b, i), h)),
        out_shape=jax.ShapeDtypeStruct((n_out, DA_HEADS * hw), F32),
        compiler_params=_cparams(3),
        name="diff_attention",
    )(da_lambda, subln_g.reshape(1, hw), p_all, p_all, p_all, p_all, p_all)


def _conv_kernel(sb_ref, sg_ref, sx_ref, gp_ref, xp_ref, gn_ref, xn_ref, w_ref, o_ref, *, n_lat_tiles, tiles_per_seq):
    i = pl.program_id(0)
    starts = (i >= n_lat_tiles) | (i % tiles_per_seq == 0)
    ends = (i >= n_lat_tiles) | (i % tiles_per_seq == tiles_per_seq - 1)
    u = sg_ref[...] * sx_ref[...]
    length = u.shape[0]
    halo_prev = jnp.where(starts, 0.0, gp_ref[7:8, :] * xp_ref[7:8, :])
    halo_next = jnp.where(ends, 0.0, gn_ref[0:1, :] * xn_ref[0:1, :])
    t = lax.broadcasted_iota(jnp.int32, u.shape, 0)
    prev = jnp.where(t == 0, halo_prev, pltpu.roll(u, 1, axis=0))
    nxt = jnp.where(t == length - 1, halo_next, pltpu.roll(u, length - 1, axis=0))
    w = w_ref[...]
    o_ref[...] = sb_ref[...] * (w[0:1] * prev + w[1:2] * u + w[2:3] * nxt)


def _short_conv(p_all, conv_w, n_out, n_lat, seq, ctx_len):
    n = p_all.shape[0]
    tm = ctx_len
    tc = 256
    ncol = SC_WIDTH // tc
    cb, cg, cx = C_SB // tc, C_SG // tc, C_SX // tc
    sub = tm // 8
    last8 = n // 8 - 1

    def main(col):
        return pl.BlockSpec((tm, tc), lambda i, j: (i, col + j))

    def halo_prev(col):
        return pl.BlockSpec((8, tc), lambda i, j: (jnp.maximum(i * sub - 1, 0), col + j))

    def halo_next(col):
        return pl.BlockSpec((8, tc), lambda i, j: (jnp.minimum((i + 1) * sub, last8), col + j))

    return pl.pallas_call(
        functools.partial(_conv_kernel, n_lat_tiles=n_lat // tm, tiles_per_seq=seq // tm),
        grid=(n_out // tm, ncol),
        in_specs=[main(cb), main(cg), main(cx), halo_prev(cg), halo_prev(cx), halo_next(cg), halo_next(cx),
                  pl.BlockSpec((3, tc), lambda i, j: (0, j))],
        out_specs=pl.BlockSpec((tm, tc), lambda i, j: (i, j)),
        out_shape=jax.ShapeDtypeStruct((n_out, SC_WIDTH), F32),
        compiler_params=_cparams(2),
        name="short_conv",
    )(p_all, p_all, p_all, p_all, p_all, p_all, p_all, conv_w)


def _bdot(a, b):
    return jnp.dot(a.astype(BF16), b.astype(BF16), preferred_element_type=F32)


def _block_unit_inverses(lmats, xor_idx):
    eye = jnp.where(xor_idx == 0, 1.0, 0.0)
    lds = [jnp.where(xor_idx < 8, l, 0.0).astype(BF16) for l in lmats]
    ld2s = [jnp.dot(ld, ld, preferred_element_type=F32).astype(BF16) for ld in lds]
    xs = [eye + ld.astype(F32) for ld in lds]
    xs = [x + jnp.dot(x.astype(BF16), ld2, preferred_element_type=F32) for x, ld2 in zip(xs, ld2s)]
    ld4s = [jnp.dot(ld2, ld2, preferred_element_type=F32).astype(BF16) for ld2 in ld2s]
    xs = [x + jnp.dot(x.astype(BF16), ld4, preferred_element_type=F32) for x, ld4 in zip(xs, ld4s)]
    b = 8
    while b < RW_CHUNK:
        offs = [jnp.where(xor_idx >= b, jnp.where(xor_idx < 2 * b, l, 0.0), 0.0).astype(BF16) for l in lmats]
        xbs = [x.astype(BF16) for x in xs]
        ts = [jnp.dot(xb, off, preferred_element_type=F32).astype(BF16) for xb, off in zip(xbs, offs)]
        xs = [x + jnp.dot(t, xb, preferred_element_type=F32) for x, t, xb in zip(xs, ts, xbs)]
        b *= 2
    return xs


def _rwkv_prepare(r, k, v, lo, w0, w2, a0, a2, k_k, k_a, r_k, gsum, direction):
    n = RW_CHUNK
    reverse = direction == 1
    dw = lo[:, direction * RW_LORA_W:(direction + 1) * RW_LORA_W]
    da = lo[:, (2 + direction) * RW_LORA_W:(3 + direction) * RW_LORA_W]
    wl = w0 + jnp.dot(jnp.tanh(dw).astype(BF16), w2.astype(BF16), preferred_element_type=F32)
    logw = -RW_DECAY_SCALE * _sigmoid(wl)
    alpha = _sigmoid(a0 + jnp.dot(da.astype(BF16), a2.astype(BF16), preferred_element_type=F32))
    keff = k * (1.0 + (alpha - 1.0) * k_a)
    kkraw = k * k_k
    kk = kkraw * lax.rsqrt(jnp.maximum(_split_dot(kkraw * kkraw, gsum), 1e-24))
    bonus = _split_dot(r * keff * r_k, gsum) * v
    ti = lax.broadcasted_iota(jnp.int32, (n, n), 0)
    si = lax.broadcasted_iota(jnp.int32, (n, n), 1)
    upto_n = (si >= ti) if reverse else (si <= ti)
    cl = jnp.dot(upto_n.astype(F32), logw, preferred_element_type=F32, precision=lax.Precision.HIGHEST)
    p_in = jnp.exp(cl)
    p_inv = jnp.exp(-cl)
    last = 0 if reverse else n - 1
    p_end = p_in[last:last + 1, :]
    b_t = kk * alpha * p_inv
    k_t = keff * p_inv
    return dict(v=v, a=-kk * jnp.exp(cl - logw), b=b_t, k=k_t, r=r * p_in, b_end=b_t * p_end, k_end=k_t * p_end,
                p_end=p_end, bonus=bonus)


def _rwkv_kernel(tok0_ref, tok1_ref, w0_ref, w2_ref, a0_ref, a2_ref, kk_ref, ka_ref, rkp_ref, gsum_ref,
                 out0_ref, out1_ref, s_ref):
    c = pl.program_id(1)

    @pl.when(c == 0)
    def _():
        s_ref[...] = jnp.zeros_like(s_ref)

    n = RW_CHUNK
    hd = RW_HEAD_DIM
    cw = RW_WIDTH
    tok_refs = (tok0_ref, tok1_ref)
    out_refs = (out0_ref, out1_ref)
    gsum = gsum_ref[...]
    prep = []
    for d in range(2):
        rr, rk, rv, lo = [tok_refs[d][:, j * cw:(j + 1) * cw] for j in range(4)]
        prep.append(_rwkv_prepare(rr, rk, rv, lo, w0_ref[d], w2_ref[d], a0_ref[d], a2_ref[d],
                                  kk_ref[...], ka_ref[...], rkp_ref[...], gsum, d))
        out_refs[d][:, cw:] = prep[d]['bonus']

    m2 = 2 * n
    t2 = lax.broadcasted_iota(jnp.int32, (m2, m2), 0)
    s2 = lax.broadcasted_iota(jnp.int32, (m2, m2), 1)
    xor_idx = t2 ^ s2
    same_head = xor_idx < n
    before = (same_head & (s2 < t2), same_head & (s2 > t2))
    upto = (same_head & (s2 <= t2), same_head & (s2 >= t2))
    first_head = lax.broadcasted_iota(jnp.int32, (n, m2), 1) < hd

    def stack(x):
        return jnp.concatenate([jnp.where(first_head, x, 0.0), jnp.where(first_head, 0.0, x)], axis=0)

    n_pairs = RW_HEADS // 2
    units = [(d, pr) for d in range(2) for pr in range(n_pairs)]

    def part(name, d, pr):
        return prep[d][name][:, pr * m2:(pr + 1) * m2]

    vss = [stack(part('v', d, pr)) for d, pr in units]
    lhss = [jnp.concatenate([stack(part('a', d, pr)), stack(part('r', d, pr))], axis=0).astype(BF16)
            for d, pr in units]
    rhss = [jnp.concatenate([stack(part('b', d, pr)), stack(part('k', d, pr))], axis=0).astype(BF16)
            for d, pr in units]
    grams = [_nt_dot(lhs, rhs) for lhs, rhs in zip(lhss, rhss)]
    s0s = [s_ref[d, pr] for d, pr in units]
    on_states = [_nt_dot(lhs, s0.astype(BF16)) for lhs, s0 in zip(lhss, s0s)]
    l_abs = [jnp.where(before[d], g[:m2, :m2], 0.0) for (d, _), g in zip(units, grams)]
    l_aks = [jnp.where(before[d], g[:m2, m2:], 0.0).astype(BF16) for (d, _), g in zip(units, grams)]
    m_alls = [jnp.concatenate([jnp.where(upto[d], g[m2:, :m2], 0.0), jnp.where(upto[d], g[m2:, m2:], 0.0)],
                              axis=1).astype(BF16) for (d, _), g in zip(units, grams)]
    u_rhss = [os_[:m2] + jnp.dot(l_ak, vs.astype(BF16), preferred_element_type=F32)
              for os_, l_ak, vs in zip(on_states, l_aks, vss)]
    tinvs = _block_unit_inverses(l_abs, xor_idx)
    us = [_bdot(tinv, u_rhs) for tinv, u_rhs in zip(tinvs, u_rhss)]
    uvs = [jnp.concatenate([u, vs], axis=0).astype(BF16) for u, vs in zip(us, vss)]
    y2s = [os_[m2:] + jnp.dot(m_all, uv, preferred_element_type=F32) for os_, m_all, uv in zip(on_states, m_alls, uvs)]
    bks = [jnp.concatenate([stack(part('b_end', d, pr)), stack(part('k_end', d, pr))], axis=0).astype(BF16)
           for d, pr in units]
    for i, (d, pr) in enumerate(units):
        s_ref[d, pr] = s0s[i] * part('p_end', d, pr) + _tn_dot(uvs[i], bks[i])
    for d in range(2):
        out_refs[d][:, :cw] = jnp.concatenate(
            [y2s[d * n_pairs + pr][:n] + y2s[d * n_pairs + pr][n:] for pr in range(n_pairs)], axis=-1)


def _rwkv_scan(p_all, p, n_lat, seq, ctx_len):
    n = p_all.shape[0]
    n_batch = n_lat // seq
    ch = RW_CHUNK
    ctx_c = ctx_len // ch
    seq_c = seq // ch
    lat0 = n_lat // ch
    cw = RW_WIDTH

    def blk_fwd(b, c):
        return jnp.where(c < ctx_c, lat0 + b * ctx_c + c, b * seq_c + (c - ctx_c))

    def blk_bwd(b, c):
        return jnp.where(c < ctx_c, lat0 + b * ctx_c + (ctx_c - 1 - c), b * seq_c + (seq_c - 1 - (c - ctx_c)))

    tok_w = 4 * cw
    assert (C_RK, C_RV, C_LORA) == (C_RR + cw, C_RR + 2 * cw, C_RR + 3 * cw) and C_RR % tok_w == 0

    def tok_spec(blk):
        return pl.BlockSpec((ch, tok_w), lambda b, c: (blk(b, c), C_RR // tok_w))

    def par_spec(shape):
        nd = len(shape)
        return pl.BlockSpec(shape, lambda b, c: (0,) * nd)

    def out_spec(blk):
        return pl.BlockSpec((ch, 2 * cw), lambda b, c: (blk(b, c), 0))

    sds = jax.ShapeDtypeStruct((n, 2 * cw), F32)
    return pl.pallas_call(
        _rwkv_kernel,
        grid=(n_batch, ctx_c + seq_c),
        in_specs=[tok_spec(blk_fwd), tok_spec(blk_bwd),
                  par_spec((2, 1, cw)), par_spec((2, RW_LORA_W, cw)), par_spec((2, 1, cw)), par_spec((2, RW_LORA_W, cw)),
                  par_spec((1, cw)), par_spec((1, cw)), par_spec((1, cw)), par_spec((cw, cw))],
        out_specs=[out_spec(blk_fwd), out_spec(blk_bwd)],
        out_shape=[sds, sds],
        scratch_shapes=[pltpu.VMEM((2, RW_HEADS // 2, 2 * RW_HEAD_DIM, 2 * RW_HEAD_DIM), F32)],
        compiler_params=_cparams(2),
        name="rwkv_scan",
    )(p_all, p_all,
      p['rw_w0'].reshape(2, 1, cw), p['rw_w2'], p['rw_a0'].reshape(2, 1, cw), p['rw_a2'],
      p['rw_k_k'].reshape(1, cw), p['rw_k_a'].reshape(1, cw), p['rw_r_k'].reshape(1, cw), _head_group_ones())


def _head_group_ones():
    head_of = jnp.arange(RW_WIDTH) // RW_HEAD_DIM
    return (head_of[:, None] == head_of[None, :]).astype(BF16)


def _split_dot(x, w_bf16):
    hi = x.astype(BF16)
    lo = (x - hi.astype(F32)).astype(BF16)
    return (jnp.dot(hi, w_bf16, preferred_element_type=F32) + jnp.dot(lo, w_bf16, preferred_element_type=F32))


def _merge_kernel(x_ref, mod_ref, gl_ref, oda_ref, osc_ref, rw0_ref, rw1_ref, lo_ref,
                  g2_ref, lng_ref, lnb_ref, gavg_ref, wb_ref, wo_ref, n2g_ref, rw_ref, rb_ref,
                  xo_ref, h2_ref, route_ref, cnt_ref):
    @pl.when(pl.program_id(0) == 0)
    def _():
        cnt_ref[...] = jnp.zeros_like(cnt_ref)

    ysum = rw0_ref[:, :RW_WIDTH] + rw1_ref[:, :RW_WIDTH]
    gavg = gavg_ref[...]
    mu = _split_dot(ysum, gavg)
    yc = ysum - mu
    var = _split_dot(yc * yc, gavg)
    yn = yc * lax.rsqrt(var + RW_GN_EPS) * lng_ref[...] + lnb_ref[...]
    dg = lo_ref[:, 4 * RW_LORA_W:4 * RW_LORA_W + RW_GATE_PAD]
    gate = jnp.dot(_sigmoid(dg).astype(BF16), g2_ref[...], preferred_element_type=F32)
    o_rw = (yn + rw0_ref[:, RW_WIDTH:] + rw1_ref[:, RW_WIDTH:]) * gate

    branches = (oda_ref[...], osc_ref[...], o_rw)
    merged = None
    for i, br in enumerate(branches):
        proj = jnp.dot(br.astype(BF16), wb_ref[i], preferred_element_type=F32)
        term = _sigmoid(gl_ref[:, i * D_MODEL:(i + 1) * D_MODEL]) * proj
        merged = term if merged is None else merged + term
    y = jnp.dot(merged.astype(BF16), wo_ref[...], preferred_element_type=F32)
    x_new = x_ref[...] + mod_ref[0, 2:3, :] * y
    xo_ref[...] = x_new
    h2 = _modnorm(x_new, n2g_ref[...], mod_ref[0, 3:4, :], mod_ref[0, 4:5, :])
    h2_ref[...] = h2
    logits = (jnp.dot(h2, rw_ref[...], preferred_element_type=F32, precision=lax.Precision.HIGHEST)
              + rb_ref[...])
    _route(logits, route_ref, cnt_ref)


def _route(logits, route_ref, cnt_ref):
    tm, ne = logits.shape
    lane = lax.broadcasted_iota(jnp.int32, logits.shape, 1)
    cur = logits
    vals, onehots, ids = [], [], []
    for _ in range(TOP_K):
        m = jnp.max(cur, axis=-1, keepdims=True)
        idx = jnp.min(jnp.where(cur == m, lane, ne), axis=-1, keepdims=True)
        hit = lane == idx
        vals.append(m)
        ids.append(idx.astype(F32))
        onehots.append(jnp.where(hit, 1.0, 0.0))
        cur = jnp.where(hit, -jnp.inf, cur)
    es = [jnp.exp(v - vals[0]) for v in vals]
    denom = es[0]
    for e in es[1:]:
        denom = denom + e
    gates = [e / denom for e in es]
    ti = lax.broadcasted_iota(jnp.int32, (tm, tm), 0)
    si = lax.broadcasted_iota(jnp.int32, (tm, tm), 1)
    earlier = jnp.where(si < ti, 1.0, 0.0).astype(BF16)
    base = cnt_ref[...]
    ranks = []
    for oh in onehots:
        prefix = jnp.dot(earlier, oh.astype(BF16), preferred_element_type=F32)
        ranks.append(jnp.sum(oh * (base + prefix), axis=-1, keepdims=True))
        base = base + jnp.sum(oh, axis=0, keepdims=True)
    cnt_ref[...] = base
    out_lane = lax.broadcasted_iota(jnp.int32, route_ref.shape, 1)
    out = jnp.zeros(route_ref.shape, F32)
    for j, col in enumerate(ids + ranks + gates):
        out = jnp.where(out_lane == j, col, out)
    route_ref[...] = out


def _merge(x_all, mod, p_all, o_da, o_sc, rw_fwd, rw_bwd, p, n_out, n_lat, seq):
    n = n_out
    tm = 256
    n_lat_tiles = n_lat // tm
    tiles_per_seq = seq // tm
    n_batch = n_lat // seq

    def seg(i):
        return jnp.where(i < n_lat_tiles, i // tiles_per_seq, n_batch)

    def row(w):
        return pl.BlockSpec((tm, w), lambda i: (i, 0))

    def const(shape):
        nd = len(shape)
        return pl.BlockSpec(shape, lambda i: (0,) * nd)

    g2 = jnp.zeros((RW_GATE_PAD, RW_WIDTH), F32).at[:RW_GATE_LORA].set(p['rw_g2']).astype(BF16)
    head_of = jnp.arange(RW_WIDTH) // RW_HEAD_DIM
    gavg = (head_of[:, None] == head_of[None, :]).astype(F32) / RW_HEAD_DIM
    return pl.pallas_call(
        _merge_kernel,
        grid=(n // tm,),
        in_specs=[row(D_MODEL),
                  pl.BlockSpec((1, N_MOD, D_MODEL), lambda i: (seg(i), 0, 0)),
                  pl.BlockSpec((tm, 3 * D_MODEL), lambda i: (i, C_GL // (3 * D_MODEL))),
                  row(512), row(512), row(2 * RW_WIDTH), row(2 * RW_WIDTH),
                  pl.BlockSpec((tm, 512), lambda i: (i, C_LORA // 512)),
                  const((RW_GATE_PAD, RW_WIDTH)), const((1, RW_WIDTH)), const((1, RW_WIDTH)),
                  const((RW_WIDTH, RW_WIDTH)),
                  const((3, 512, D_MODEL)), const((D_MODEL, D_MODEL)), const((1, D_MODEL)),
                  const((D_MODEL, N_EXPERTS)), const((1, N_EXPERTS))],
        out_specs=[row(D_MODEL),
                   row(D_MODEL),
                   row(128),
                   const((1, N_EXPERTS))],
        out_shape=[jax.ShapeDtypeStruct((n, D_MODEL), F32),
                   jax.ShapeDtypeStruct((n, D_MODEL), F32),
                   jax.ShapeDtypeStruct((n, 128), F32),
                   jax.ShapeDtypeStruct((1, N_EXPERTS), F32)],
        compiler_params=_cparams(1),
        name="merge_router",
    )(x_all, mod, p_all, o_da, o_sc, rw_fwd, rw_bwd, p_all,
      g2, p['rw_lnx_g'].reshape(1, RW_WIDTH), p['rw_lnx_b'].reshape(1, RW_WIDTH), gavg.astype(BF16),
      p['w_branch'].astype(BF16), p['w_out'].astype(BF16), p['norm2_g'].reshape(1, D_MODEL),
      p['router_w'], p['router_b'].reshape(1, N_EXPERTS))


def _dispatch_kernel(pos_ref, h2_ref, xb_in, xb_out, sem, *, tm):
    del xb_in
    i = pl.program_id(0)

    def copy(t, row):
        return pltpu.make_async_copy(h2_ref.at[pl.ds(t, 1), :], xb_out.at[pl.ds(row, 1), :], sem.at[0])

    def issue(it, carry):
        for u in range(DMA_UNROLL // TOP_K):
            t = it * (DMA_UNROLL // TOP_K) + u
            base = (i * tm + t) * TOP_K
            for k in range(TOP_K):
                copy(t, pos_ref[base + k]).start(priority=k % 2)
        return carry

    def drain(it, carry):
        for _ in range(DMA_UNROLL):
            copy(0, 0).wait()
        return carry

    lax.fori_loop(0, tm * TOP_K // DMA_UNROLL, issue, 0)
    lax.fori_loop(0, tm * TOP_K // DMA_UNROLL, drain, 0)


def _dispatch(h2, pos, n_rows):
    n_tok = pos.shape[0] // TOP_K
    tm = 256
    grid_spec = pltpu.PrefetchScalarGridSpec(
        num_scalar_prefetch=1,
        grid=(n_tok // tm,),
        in_specs=[pl.BlockSpec((tm, D_MODEL), lambda i, pos: (i, 0)),
                  pl.BlockSpec(memory_space=pl.ANY)],
        out_specs=pl.BlockSpec(memory_space=pl.ANY),
        scratch_shapes=[pltpu.SemaphoreType.DMA((1,))],
    )
    return pl.pallas_call(
        functools.partial(_dispatch_kernel, tm=tm),
        grid_spec=grid_spec,
        out_shape=jax.ShapeDtypeStruct((n_rows, D_MODEL), F32),
        input_output_aliases={2: 0},
        compiler_params=pltpu.CompilerParams(dimension_semantics=("arbitrary",), vmem_limit_bytes=VMEM_LIMIT,
                                             disable_bounds_checks=True),
        name="moe_dispatch",
    )(pos, h2, jnp.zeros((n_rows, D_MODEL), F32))


def _expert_kernel(be_ref, nvalid_ref, x_ref, w1_ref, b1_ref, w2_ref, b2_ref, o_ref, w1b, w2b):
    i = pl.program_id(0)
    nvalid = nvalid_ref[0]

    @pl.when(i < nvalid)
    def _():
        @pl.when((i == 0) | (be_ref[i] != be_ref[jnp.maximum(i - 1, 0)]))
        def _():
            w1b[...] = w1_ref[0, 0].astype(BF16)
            w2b[...] = w2_ref[0, 0].astype(BF16)

        hid = jnp.dot(x_ref[...].astype(BF16), w1b[...], preferred_element_type=F32) + b1_ref[0, 0]
        glu = jnp.minimum(hid[:, :D_FF], SWIGLU_LIMIT)
        lin = jnp.clip(hid[:, D_FF:], -SWIGLU_LIMIT, SWIGLU_LIMIT)
        act = glu * _sigmoid(SWIGLU_ALPHA * glu) * (lin + 1.0)
        o_ref[...] = jnp.dot(act.astype(BF16), w2b[...], preferred_element_type=F32) + b2_ref[0, 0]

    @pl.when(i >= nvalid)
    def _():
        o_ref[...] = jnp.zeros_like(o_ref)


def _expert_ffn(xb, block_expert, nvalid, li, w1, b1, w2, b2):
    n_rows = xb.shape[0]
    mb = MOE_BLOCK
    depth = w1.shape[0]
    grid_spec = pltpu.PrefetchScalarGridSpec(
        num_scalar_prefetch=2,
        grid=(n_rows // mb,),
        in_specs=[pl.BlockSpec((mb, D_MODEL), lambda i, be, nv: (i, 0)),
                  pl.BlockSpec((1, 1, D_MODEL, 2 * D_FF), lambda i, be, nv: (li, be[i], 0, 0)),
                  pl.BlockSpec((1, 1, 1, 2 * D_FF), lambda i, be, nv: (li, be[i], 0, 0)),
                  pl.BlockSpec((1, 1, D_FF, D_MODEL), lambda i, be, nv: (li, be[i], 0, 0)),
                  pl.BlockSpec((1, 1, 1, D_MODEL), lambda i, be, nv: (li, be[i], 0, 0))],
        out_specs=pl.BlockSpec((mb, D_MODEL), lambda i, be, nv: (i, 0)),
        scratch_shapes=[pltpu.VMEM((D_MODEL, 2 * D_FF), BF16), pltpu.VMEM((D_FF, D_MODEL), BF16)],
    )
    return pl.pallas_call(
        _expert_kernel,
        grid_spec=grid_spec,
        out_shape=jax.ShapeDtypeStruct((n_rows, D_MODEL), F32),
        compiler_params=_cparams(1),
        name="expert_ffn",
    )(block_expert, nvalid, xb, w1, b1.reshape(depth, N_EXPERTS, 1, 2 * D_FF), w2,
      b2.reshape(depth, N_EXPERTS, 1, D_MODEL))


def _combine_kernel(pos_ref, yb_hbm, x_ref, route_ref, mod_ref, fg_ref, o_ref, buf, sem, *, tm, n_tiles, final_norm):
    i = pl.program_id(0)

    def copy(idx, slot, r):
        return pltpu.make_async_copy(yb_hbm.at[pl.ds(idx, 1), :], buf.at[slot, pl.ds(r, 1), :], sem.at[slot])

    def issue(tile, slot):
        def body(it, carry):
            for u in range(DMA_UNROLL):
                r = it * DMA_UNROLL + u
                copy(pos_ref[tile * (tm * TOP_K) + r], slot, r).start(priority=u % 2)
            return carry
        lax.fori_loop(0, tm * TOP_K // DMA_UNROLL, body, 0)

    @pl.when(i == 0)
    def _():
        issue(0, 0)

    @pl.when(i + 1 < n_tiles)
    def _():
        issue(i + 1, (i + 1) % 2)

    slot = i % 2

    def wait_body(it, carry):
        for _ in range(DMA_UNROLL):
            copy(0, slot, 0).wait()
        return carry
    lax.fori_loop(0, tm * TOP_K // DMA_UNROLL, wait_body, 0)

    f = route_ref[:, 2 * TOP_K:2 * TOP_K + 1] * buf[slot, pl.ds(0, tm), :]
    for kk in range(1, TOP_K):
        f = f + route_ref[:, 2 * TOP_K + kk:2 * TOP_K + kk + 1] * buf[slot, pl.ds(kk * tm, tm), :]
    x_new = x_ref[...] + mod_ref[0, 5:6, :] * f
    if final_norm:
        ms = jnp.mean(x_new * x_new, axis=-1, keepdims=True)
        x_new = x_new * lax.rsqrt(ms + NORM_EPS) * fg_ref[...]
    o_ref[...] = x_new


def _combine(yb, pos, x_all, route, mod, final_g, n_out, n_lat, seq, final_norm):
    tm = 256
    n_tiles = n_out // tm
    n_lat_tiles = n_lat // tm
    tiles_per_seq = seq // tm
    n_batch = n_lat // seq

    def seg(i):
        return jnp.where(i < n_lat_tiles, i // tiles_per_seq, n_batch)

    grid_spec = pltpu.PrefetchScalarGridSpec(
        num_scalar_prefetch=1,
        grid=(n_tiles,),
        in_specs=[pl.BlockSpec(memory_space=pl.ANY),
                  pl.BlockSpec((tm, D_MODEL), lambda i, pos: (i, 0)),
                  pl.BlockSpec((tm, 128), lambda i, pos: (i, 0)),
                  pl.BlockSpec((1, N_MOD, D_MODEL), lambda i, pos: (seg(i), 0, 0)),
                  pl.BlockSpec((1, D_MODEL), lambda i, pos: (0, 0))],
        out_specs=pl.BlockSpec((tm, D_MODEL), lambda i, pos: (i, 0)),
        scratch_shapes=[pltpu.VMEM((2, tm * TOP_K, D_MODEL), F32), pltpu.SemaphoreType.DMA((2,))],
    )
    return pl.pallas_call(
        functools.partial(_combine_kernel, tm=tm, n_tiles=n_tiles, final_norm=final_norm),
        grid_spec=grid_spec,
        out_shape=jax.ShapeDtypeStruct((n_out, D_MODEL), F32),
        compiler_params=pltpu.CompilerParams(dimension_semantics=("arbitrary",), vmem_limit_bytes=VMEM_LIMIT,
                                             disable_bounds_checks=True),
        name="moe_combine",
    )(pos, yb, x_all, route, mod, final_g.reshape(1, D_MODEL))


def _routing(route, counts):
    n_tok = route.shape[0]
    n_assign = n_tok * TOP_K
    counts = counts.reshape(N_EXPERTS).astype(jnp.int32)
    padded = (counts + MOE_BLOCK - 1) // MOE_BLOCK * MOE_BLOCK
    end_padded = jnp.cumsum(padded)
    start_padded = end_padded - padded
    n_rows = -(-n_assign // MOE_BLOCK) * MOE_BLOCK + N_EXPERTS * MOE_BLOCK
    n_blocks = n_rows // MOE_BLOCK
    block_start = jnp.arange(n_blocks, dtype=jnp.int32) * MOE_BLOCK
    block_expert = jnp.minimum(jnp.sum((end_padded[None, :] <= block_start[:, None]).astype(jnp.int32), axis=1),
                               N_EXPERTS - 1).astype(jnp.int32)
    nvalid = (end_padded[-1] // MOE_BLOCK).astype(jnp.int32).reshape(1)
    expert = route[:, :TOP_K].astype(jnp.int32)
    rank = route[:, TOP_K:2 * TOP_K].astype(jnp.int32)
    onehot = expert[:, :, None] == jnp.arange(N_EXPERTS, dtype=jnp.int32)[None, None, :]
    pos = rank + jnp.sum(jnp.where(onehot, start_padded[None, None, :], 0), axis=-1)
    return block_expert, nvalid, pos.reshape(-1).astype(jnp.int32), n_rows


def _layer(x_all, mod, li, p, tables, final_g, n_lat, seq, ctx_len, last):
    cos_t, sin_t = tables
    lam_init = 0.8 - 0.6 * math.exp(-0.3 * li)
    pad = jnp.zeros((D_MODEL, P_COLS - C_LORA - 4 * RW_LORA_W - RW_GATE_LORA), F32)
    w_in = p['w_in']
    w_all = jnp.concatenate(
        [w_in[:, 4608:], w_in[:, :4608], p['rw_w1'][0], p['rw_w1'][1], p['rw_a1'][0], p['rw_a1'][1],
         p['rw_g1'], pad], axis=1).astype(BF16)
    p_all = _projection(x_all, mod, p['norm1_g'], w_all, cos_t, sin_t, n_lat, seq)
    o_da = _attention(p_all, p['da_lambda'], p['da_subln_g'], lam_init, n_lat, seq, ctx_len, not last)
    n_tok = n_lat if last else x_all.shape[0]
    o_sc = _short_conv(p_all, p['conv_w'], n_tok, n_lat, seq, ctx_len)
    rw_fwd, rw_bwd = _rwkv_scan(p_all, p, n_lat, seq, ctx_len)
    x_mid, h2, route, counts = _merge(x_all, mod, p_all, o_da, o_sc, rw_fwd, rw_bwd, p, n_tok, n_lat, seq)
    block_expert, nvalid, pos, n_rows = _routing(route, counts)
    xb = _dispatch(h2, pos, n_rows)
    yb = _expert_ffn(xb, block_expert, nvalid, li, p['exp_w1'], p['exp_b1'], p['exp_w2'], p['exp_b2'])
    tm = 256
    pos_by_slot = pos.reshape(n_tok // tm, tm, TOP_K).transpose(0, 2, 1).reshape(-1)
    return _combine(yb, pos_by_slot, x_mid, route, mod, final_g, n_tok, n_lat, seq, last)


def kernel(x, c, ctx, c_ctx, w_mod, b_mod, norm1_g, norm2_g, w_in, da_lambda, da_subln_g, conv_w, rw_w0, rw_w1, rw_w2, rw_a0, rw_a1, rw_a2, rw_g1, rw_g2, rw_k_k, rw_k_a, rw_r_k, rw_lnx_g, rw_lnx_b, w_branch, w_out, router_w, router_b, exp_w1, exp_b1, exp_w2, exp_b2, final_g):
    n_batch, seq, d = x.shape
    ctx_len = ctx.shape[1]
    depth = w_in.shape[0]
    n_lat = n_batch * seq
    assert d == D_MODEL and n_batch <= 8
    proj_tm = max(tm for tm in PROJ_TMS if seq % tm == 0 and (n_batch * ctx_len) % tm == 0)
    assert seq % ctx_len == 0 and ctx_len == 256
    x_all = jnp.concatenate([x.reshape(n_lat, d), ctx.reshape(n_batch * ctx_len, d)], axis=0)
    c_all = jnp.zeros((16, d), F32).at[:n_batch].set(c).at[n_batch].set(c_ctx)
    tables = _rope_tables(seq, proj_tm)
    for li in range(depth):
        p = {
            'norm1_g': norm1_g[li], 'norm2_g': norm2_g[li], 'w_in': w_in[li], 'da_lambda': da_lambda[li],
            'da_subln_g': da_subln_g[li], 'conv_w': conv_w[li],
            'rw_w0': rw_w0[li], 'rw_w1': rw_w1[li], 'rw_w2': rw_w2[li],
            'rw_a0': rw_a0[li], 'rw_a1': rw_a1[li], 'rw_a2': rw_a2[li],
            'rw_g1': rw_g1[li], 'rw_g2': rw_g2[li], 'rw_k_k': rw_k_k[li], 'rw_k_a': rw_k_a[li],
            'rw_r_k': rw_r_k[li], 'rw_lnx_g': rw_lnx_g[li], 'rw_lnx_b': rw_lnx_b[li],
            'w_branch': w_branch[li], 'w_out': w_out[li], 'router_w': router_w[li], 'router_b': router_b[li],
            'exp_w1': exp_w1, 'exp_b1': exp_b1, 'exp_w2': exp_w2, 'exp_b2': exp_b2,
        }
        mod = _modulation(c_all, w_mod[li], b_mod[li]).reshape(16, N_MOD, D_MODEL)
        x_all = _layer(x_all, mod, li, p, tables, final_g, n_lat, seq, ctx_len, li == depth - 1)
    return x_all.reshape(n_batch, seq, d)
```
